```python
import jax
import jax.numpy as jnp
from jax import lax
import numpy as np

D_MODEL = 1024
BATCH = 8
SEQ = 4096
DEPTH = 2

HEAD_DIM = 64
H_A = 4
H_B = 4
H_C = 8
D_MIX = (H_A + H_B + H_C) * HEAD_DIM
IDX_HEADS = 8
IDX_DIM = 64
TOPK_MAX = 256
DILATIONS = ((128, 1), (512, 4), (2048, 16))
Q_BLOCK = 128
D_FF = 2816
N_MOD = 9
IN_COLS = 3 * D_MIX + IDX_HEADS * IDX_DIM + IDX_DIM + IDX_HEADS
EPS = 1e-6

kernel_name = 'hybrid_dsa_stickbreak_dilated_macaron_adaln'


def rms_norm(x, gain):
    xf = x.astype(jnp.float32)
    y = xf * lax.rsqrt(jnp.mean(xf * xf, axis=-1, keepdims=True) + EPS)
    return (y * gain.astype(jnp.float32)).astype(x.dtype)


def modulate(h, shift, scale):
    return h * (1 + scale) + shift


def swiglu(h, w_gate, w_up, w_down):
    g = jnp.einsum('bsd,df->bsf', h, w_gate)
    u = jnp.einsum('bsd,df->bsf', h, w_up)
    return jnp.einsum('bsf,fd->bsd', jax.nn.silu(g) * u, w_down)


def alibi_slopes():
    n = H_A + H_C
    slopes = jnp.exp2(-8.0 * jnp.arange(1, n + 1, dtype=jnp.float32) / n)
    idx = np.arange(n)
    return slopes[idx[idx % 3 == 2]], slopes[idx[idx % 3 != 2]]


def to_blocks(a):
    b, s = a.shape[0], a.shape[1]
    return jnp.moveaxis(a.reshape(b, s // Q_BLOCK, Q_BLOCK, *a.shape[2:]), 1, 0)


def from_blocks(a):
    nb, b = a.shape[0], a.shape[1]
    return jnp.moveaxis(a, 0, 1).reshape(b, nb * Q_BLOCK, *a.shape[3:])


def dsa_attention(q, k, v, qi, ki, wi, slopes, topk):
    s_len = q.shape[1]
    kpos = jnp.arange(s_len)
    scale = HEAD_DIM ** -0.5
    idx_scale = (IDX_DIM ** -0.5) * (IDX_HEADS ** -0.5)

    def block(args):
        qb, qib, wib, t0 = args
        tpos = t0 + jnp.arange(Q_BLOCK)
        rel = jnp.einsum('bthe,bse->bhts', qib, ki).astype(jnp.float32)
        score = jnp.einsum('bth,bhts->bts', wib.astype(jnp.float32), jax.nn.relu(rel)) * idx_scale
        score = jnp.where(kpos[None, None, :] <= tpos[None, :, None], score, -jnp.inf)
        _, sel = lax.top_k(score, topk)
        kg = jax.vmap(lambda kk, ii: kk[ii])(k, sel)
        vg = jax.vmap(lambda vv, ii: vv[ii])(v, sel)
        dist = tpos[None, :, None] - sel
        logits = jnp.einsum('bthd,btkhd->bhtk', qb, kg).astype(jnp.float32) * scale
        logits = logits - slopes[None, :, None, None] * dist[:, None].astype(jnp.float32)
        logits = jnp.where((dist >= 0)[:, None], logits, -jnp.inf)
        p = jax.nn.softmax(logits, axis=-1).astype(v.dtype)
        return jnp.einsum('bhtk,btkhd->bthd', p, vg)

    nb = s_len // Q_BLOCK
    out = lax.map(block, (to_blocks(q), to_blocks(qi), to_blocks(wi), jnp.arange(nb) * Q_BLOCK))
    return from_blocks(out)


def stick_breaking_attention(q, k, v):
    s_len = q.shape[1]
    kpos = jnp.arange(s_len)
    scale = HEAD_DIM ** -0.5

    def block(args):
        qb, t0 = args
        tpos = t0 + jnp.arange(Q_BLOCK)
        z = jnp.einsum('bthd,bshd->bhts', qb, k).astype(jnp.float32) * scale
        strict = kpos[None, :] < tpos[:, None]
        log_1m = jnp.where(strict, jax.nn.log_sigmoid(-z), 0.0)
        log_stay = lax.cumsum(log_1m, axis=3, reverse=True) - log_1m
        a = jnp.where(strict, jnp.exp(jax.nn.log_sigmoid(z) + log_stay), 0.0)
        return jnp.einsum('bhts,bshd->bthd', a.astype(v.dtype), v)

    nb = s_len // Q_BLOCK
    out = lax.map(block, (to_blocks(q), jnp.arange(nb) * Q_BLOCK))
    return from_blocks(out)


def dilated_attention(q, k, v, slopes):
    s_len = q.shape[1]
    scale = HEAD_DIM ** -0.5

    def block(args):
        qb, t0 = args
        tpos = t0 + jnp.arange(Q_BLOCK)
        lses, outs = [], []
        for window, dil in DILATIONS:
            dist = jnp.arange(window // dil + 1) * dil
            kidx = tpos[:, None] - dist[None, :]
            valid = kidx >= 0
            kidx = jnp.maximum(kidx, 0)
            kg = k[:, kidx]
            vg = v[:, kidx]
            logits = jnp.einsum('bthd,btnhd->bhtn', qb, kg).astype(jnp.float32) * scale
            logits = logits - slopes[:, None, None] * dist.astype(jnp.float32)[None, None, :]
            logits = jnp.where(valid[None, None], logits, -jnp.inf)
            m = jnp.max(logits, axis=-1, keepdims=True)
            p = jnp.exp(logits - m)
            l = jnp.sum(p, axis=-1, keepdims=True)
            outs.append(jnp.einsum('bhtn,btnhd->bthd', (p / l).astype(v.dtype), vg))
            lses.append(m + jnp.log(l))
        w = jax.nn.softmax(jnp.stack(lses, axis=0), axis=0)
        w = jnp.transpose(w, (0, 1, 3, 2, 4)).astype(v.dtype)
        return sum(w[i] * outs[i] for i in range(len(DILATIONS)))

    nb = s_len // Q_BLOCK
    out = lax.map(block, (to_blocks(q), jnp.arange(nb) * Q_BLOCK))
    return from_blocks(out)


def token_mixing(h, w_in, qn_a, kn_a, qn_c, kn_c, g_out, w_out, slopes_a, slopes_c):
    b, s_len, _ = h.shape
    topk = min(TOPK_MAX, s_len // 4)
    proj = jnp.einsum('bsd,de->bse', h, w_in)
    sizes = [H_A * HEAD_DIM] * 3 + [IDX_HEADS * IDX_DIM, IDX_DIM, IDX_HEADS] + [H_B * HEAD_DIM] * 3 + [H_C * HEAD_DIM] * 3
    offsets = np.cumsum(sizes)[:-1].tolist()
    qa, ka, va, qi, ki, wi, qb, kb, vb, qc, kc, vc = jnp.split(proj, offsets, axis=-1)
    heads = lambda t, n: t.reshape(b, s_len, n, HEAD_DIM)
    o_a = dsa_attention(rms_norm(heads(qa, H_A), qn_a), rms_norm(heads(ka, H_A), kn_a), heads(va, H_A),
                        qi.reshape(b, s_len, IDX_HEADS, IDX_DIM), ki, wi, slopes_a, topk)
    o_b = stick_breaking_attention(heads(qb, H_B), heads(kb, H_B), heads(vb, H_B))
    o_c = dilated_attention(rms_norm(heads(qc, H_C), qn_c), rms_norm(heads(kc, H_C), kn_c), heads(vc, H_C), slopes_c)
    wa, wb = H_A * HEAD_DIM, H_B * HEAD_DIM
    g_a, g_b, g_c = jnp.split(g_out, [wa, wa + wb])
    y = jnp.concatenate([rms_norm(o_a.reshape(b, s_len, -1), g_a),
                         rms_norm(o_b.reshape(b, s_len, -1), g_b),
                         rms_norm(o_c.reshape(b, s_len, -1), g_c)], axis=-1)
    return jnp.einsum('bse,ed->bsd', y, w_out)


def setup_inputs(seed: int = 0) -> dict:
    key = jax.random.key(seed)
    ks = jax.random.split(key, 24)
    f32 = jnp.float32
    L, D = DEPTH, D_MODEL

    def nrm(k, shape, scale):
        return jax.random.normal(k, shape, f32) * scale

    def gain(k, shape):
        return 1.0 + 0.05 * jax.random.normal(k, shape, f32)

    return {
        'x': nrm(ks[0], (BATCH, SEQ, D), 1.0),
        'c': nrm(ks[1], (BATCH, D), 1.0),
        'w_ada': nrm(ks[2], (L, D, N_MOD * D), 0.5 * D ** -0.5),
        'b_ada': nrm(ks[3], (L, N_MOD * D), 0.02),
        'norm_ffn1': gain(ks[4], (L, D)),
        'w_gate1': nrm(ks[5], (L, D, D_FF), D ** -0.5),
        'w_up1': nrm(ks[6], (L, D, D_FF), D ** -0.5),
        'w_down1': nrm(ks[7], (L, D_FF, D), D_FF ** -0.5),
        'norm_mix': gain(ks[8], (L, D)),
        'w_in': nrm(ks[9], (L, D, IN_COLS), D ** -0.5),
        'qn_a': gain(ks[10], (L, HEAD_DIM)),
        'kn_a': gain(ks[11], (L, HEAD_DIM)),
        'qn_c': gain(ks[12], (L, HEAD_DIM)),
        'kn_c': gain(ks[13], (L, HEAD_DIM)),
        'g_out': gain(ks[14], (L, D_MIX)),
        'w_out': nrm(ks[15], (L, D_MIX, D), D_MIX ** -0.5),
        'norm_ffn2': gain(ks[16], (L, D)),
        'w_gate2': nrm(ks[17], (L, D, D_FF), D ** -0.5),
        'w_up2': nrm(ks[18], (L, D, D_FF), D ** -0.5),
        'w_down2': nrm(ks[19], (L, D_FF, D), D_FF ** -0.5),
    }


def reference(x, c, w_ada, b_ada, norm_ffn1, w_gate1, w_up1, w_down1, norm_mix, w_in, qn_a, kn_a, qn_c, kn_c,
              g_out, w_out, norm_ffn2, w_gate2, w_up2, w_down2):
    slopes_a, slopes_c = alibi_slopes()
    cond = jax.nn.silu(c)
    for l in range(DEPTH):
        mod = jnp.einsum('bd,de->be', cond, w_ada[l]) + b_ada[l]
        sh1, sc1, g1, sh2, sc2, g2, sh3, sc3, g3 = [m[:, None, :] for m in jnp.split(mod, N_MOD, axis=-1)]
        h = modulate(rms_norm(x, norm_ffn1[l]), sh1, sc1)
        x = x + 0.5 * g1 * swiglu(h, w_gate1[l], w_up1[l], w_down1[l])
        h = modulate(rms_norm(x, norm_mix[l]), sh2, sc2)
        x = x + g2 * token_mixing(h, w_in[l], qn_a[l], kn_a[l], qn_c[l], kn_c[l], g_out[l], w_out[l], slopes_a, slopes_c)
        h = modulate(rms_norm(x, norm_ffn2[l]), sh3, sc3)
        x = x + 0.5 * g3 * swiglu(h, w_gate2[l], w_up2[l], w_down2[l])
    return x
```

```python
import functools

import numpy as np
import jax
import jax.numpy as jnp
from jax import lax
from jax.experimental import pallas as pl
from jax.experimental.pallas import tpu as pltpu

HEAD_DIM = 64
H_A = 4
H_B = 4
H_C = 8
IDX_HEADS = 8
IDX_DIM = 64
TOPK_MAX = 256
DILATIONS = ((128, 1), (512, 4), (2048, 16))
N_MOD = 9
EPS = 1e-6

TQ = 128
NEG = -1e30
INT_MIN = -2 ** 31
NEG_INF_KEY = -2139095041
V7X_VMEM_BYTES = 64 * 1024 * 1024

f32 = jnp.float32
bf16 = jnp.bfloat16
i32 = jnp.int32

_NT = (((1,), (1,)), ((), ()))


def _params(sem, vmem_mb):
    return pltpu.CompilerParams(dimension_semantics=sem,
                                vmem_limit_bytes=min(vmem_mb * 1024 * 1024, V7X_VMEM_BYTES - (4 << 20)))


def _alibi_slopes():
    n = H_A + H_C
    s = np.exp2(-8.0 * np.arange(1, n + 1, dtype=np.float32) / np.float32(n)).astype(np.float32)
    idx = np.arange(n)
    return [float(v) for v in s[idx % 3 == 2]], [float(v) for v in s[idx % 3 != 2]]


def _resident(shape):
    nd = len(shape)
    return pl.BlockSpec(shape, lambda *_: (0,) * nd, pipeline_mode=pl.Buffered(1))


def _mod_kernel(c_ref, w_ref, b_ref, o_ref):
    c = c_ref[...]
    cond = c * jax.nn.sigmoid(c)
    o_ref[0] = jnp.dot(cond.astype(bf16), w_ref[0].astype(bf16), preferred_element_type=f32) + b_ref[0]


def _modulation(c, w_ada, b_ada):
    L, D, ND = w_ada.shape
    B = c.shape[0]
    tn = D
    return pl.pallas_call(
        _mod_kernel,
        out_shape=jax.ShapeDtypeStruct((L, B, ND), f32),
        grid=(L, ND // tn),
        in_specs=[pl.BlockSpec((B, D), lambda l, n: (0, 0)),
                  pl.BlockSpec((1, D, tn), lambda l, n: (l, 0, n)),
                  pl.BlockSpec((1, 1, tn), lambda l, n: (l, 0, n))],
        out_specs=pl.BlockSpec((1, B, tn), lambda l, n: (l, 0, n)),
        compiler_params=_params(("arbitrary", "arbitrary"), 32),
        name="adaln_mod",
    )(c, w_ada, b_ada.reshape(L, 1, ND))


def _norm_mod(x, nw, sc, sh):
    ms = jnp.mean(x * x, axis=-1, keepdims=True)
    y = x * lax.rsqrt(ms + EPS) * nw
    return y * (1.0 + sc) + sh


def _ffn_kernel(x_ref, sh_ref, sc_ref, gt_ref, nw_ref, wg_ref, wu_ref, wd_ref, o_ref, *, n_chunks):
    x = x_ref[0]
    h = _norm_mod(x, nw_ref[...], sc_ref[0], sh_ref[0]).astype(bf16)
    F = wg_ref.shape[1]
    fc = F // n_chunks
    y = None
    for c in range(n_chunks):
        g = jnp.dot(h, wg_ref[:, c * fc:(c + 1) * fc], preferred_element_type=f32)
        u = jnp.dot(h, wu_ref[:, c * fc:(c + 1) * fc], preferred_element_type=f32)
        a = (g * jax.nn.sigmoid(g) * u).astype(bf16)
        part = jnp.dot(a, wd_ref[c * fc:(c + 1) * fc, :], preferred_element_type=f32)
        y = part if y is None else y + part
    o_ref[0] = x + (0.5 * gt_ref[0]) * y


def _ffn(x, sh, sc, gt, nw, wg, wu, wd, *, tm, n_chunks):
    B, S, D = x.shape
    F = wg.shape[1]
    vec = pl.BlockSpec((1, 1, D), lambda b, j: (b, 0, 0))
    return pl.pallas_call(
        functools.partial(_ffn_kernel, n_chunks=n_chunks),
        out_shape=jax.ShapeDtypeStruct((B, S, D), f32),
        grid=(B, S // tm),
        in_specs=[pl.BlockSpec((1, tm, D), lambda b, j: (b, j, 0)), vec, vec, vec,
                  _resident((1, D)), _resident((D, F)), _resident((D, F)), _resident((F, D))],
        out_specs=pl.BlockSpec((1, tm, D), lambda b, j: (b, j, 0)),
        compiler_params=_params(("parallel", "parallel"), 56),
        name="ffn_swiglu",
    )(x, sh, sc, gt, nw, wg, wu, wd)


def _head_rms(x, gain_row, scale):
    W = x.shape[1]
    outs = []
    r = lax.broadcasted_iota(i32, (256, 256), 0) >> 6
    c = lax.broadcasted_iota(i32, (256, 256), 1) >> 6
    bd = jnp.where(r == c, 1.0, 0.0).astype(bf16)
    for s in range(W // 256):
        xs = x[:, s * 256:(s + 1) * 256]
        sq = xs * xs
        hi = sq.astype(bf16)
        lo = (sq - hi.astype(f32)).astype(bf16)
        ss = jnp.dot(hi, bd, preferred_element_type=f32) + jnp.dot(lo, bd, preferred_element_type=f32)
        y = xs * lax.rsqrt(ss * (1.0 / HEAD_DIM) + EPS) * gain_row[:, s * 256:(s + 1) * 256]
        outs.append(y * scale if scale != 1.0 else y)
    return outs[0] if len(outs) == 1 else jnp.concatenate(outs, axis=-1)


def _proj_kernel(x_ref, sh_ref, sc_ref, nw_ref, w_ref, qna_ref, kna_ref, qnc_ref, knc_ref,
                 a_ref, b_ref, c_ref, qi_ref, kib_ref, kw_ref):
    x = x_ref[0]
    h = _norm_mod(x, nw_ref[...], sc_ref[0], sh_ref[0]).astype(bf16)
    WA, WB, WC = H_A * HEAD_DIM, H_B * HEAD_DIM, H_C * HEAD_DIM
    qscale = HEAD_DIM ** -0.5
    o = 0

    def mm(width):
        nonlocal o
        r = jnp.dot(h, w_ref[:, o:o + width], preferred_element_type=f32)
        o += width
        return r

    pa = mm(3 * WA)
    a_ref[0] = jnp.concatenate([_head_rms(pa[:, :WA], qna_ref[...], qscale),
                                _head_rms(pa[:, WA:2 * WA], kna_ref[...], 1.0),
                                pa[:, 2 * WA:]], axis=-1).astype(bf16)
    pb = mm(3 * WB)
    b_ref[0] = jnp.concatenate([pb[:, :WB] * qscale, pb[:, WB:]], axis=-1).astype(bf16)
    pc = mm(3 * WC)
    c_ref[0] = jnp.concatenate([_head_rms(pc[:, :WC], qnc_ref[...], qscale),
                                _head_rms(pc[:, WC:2 * WC], knc_ref[...], 1.0),
                                pc[:, 2 * WC:]], axis=-1).astype(bf16)
    qi_ref[0] = mm(IDX_HEADS * IDX_DIM).astype(bf16)
    kw = mm(128)
    kib_ref[0] = kw.astype(bf16)
    kw_ref[0] = kw


def _proj(x, sh, sc, nw, w2, qna, kna, qnc, knc, *, tm):
    B, S, D = x.shape
    WA, WB, WC = H_A * HEAD_DIM, H_B * HEAD_DIM, H_C * HEAD_DIM
    WI = IDX_HEADS * IDX_DIM
    vec = pl.BlockSpec((1, 1, D), lambda b, j: (b, 0, 0))
    widths = (3 * WA, 3 * WB, 3 * WC, WI, 128, 128)
    dtypes = (bf16, bf16, bf16, bf16, bf16, f32)
    return pl.pallas_call(
        _proj_kernel,
        out_shape=[jax.ShapeDtypeStruct((B, S, w), dt) for w, dt in zip(widths, dtypes)],
        grid=(B, S // tm),
        in_specs=[pl.BlockSpec((1, tm, D), lambda b, j: (b, j, 0)), vec, vec,
                  _resident((1, D)), _resident(w2.shape),
                  _resident((1, WA)), _resident((1, WA)), _resident((1, WC)), _resident((1, WC))],
        out_specs=[pl.BlockSpec((1, tm, w), lambda b, j: (b, j, 0)) for w in widths],
        compiler_params=_params(("parallel", "parallel"), 48),
        name="mixer_in_proj",
    )(x, sh, sc, nw, w2, qna, kna, qnc, knc)


def _dsa_kernel(q_ref, k_ref, v_ref, qi_ref, ki_ref, w_ref, o_ref, key_ref, bias_ref, j_ref,
                *, topk, slopes, ck):
    S = k_ref.shape[1]
    i = pl.program_id(1)
    t0 = i * TQ
    nck = lax.div(t0 + TQ + ck - 1, ck)
    lt = ck // 128
    tpos = t0 + lax.broadcasted_iota(i32, (TQ, 1), 0)
    lane_pos = lax.broadcasted_iota(i32, (1, ck), 1)
    idx_scale = (IDX_DIM ** -0.5) * (IDX_HEADS ** -0.5)

    def fold(m):
        r = m[:, 0:128]
        for t in range(1, lt):
            r = r + m[:, t * 128:(t + 1) * 128]
        return r

    def count(pred_fn):
        def body(c, acc):
            m = pred_fn(key_ref[c], c * ck + lane_pos)
            return acc + fold(jnp.where(m, 1, 0).astype(i32))
        acc = lax.fori_loop(0, nck, body, jnp.zeros((TQ, 128), i32))
        return jnp.sum(acc.astype(f32), axis=-1, keepdims=True)

    qi = qi_ref[0]
    wts = w_ref[0]

    def score_body(c, carry):
        c0 = pl.multiple_of(c * ck, ck)
        kic = ki_ref[0, pl.ds(c0, ck), :][:, :IDX_DIM]
        score = jnp.zeros((TQ, ck), f32)
        for h in range(IDX_HEADS):
            rel = lax.dot_general(qi[:, h * IDX_DIM:(h + 1) * IDX_DIM], kic, _NT, preferred_element_type=f32)
            score = score + wts[:, IDX_DIM + h:IDX_DIM + h + 1] * jnp.maximum(rel, 0.0)
        score = score * idx_scale
        score = jnp.where(score == 0.0, 0.0, score)
        score = jnp.where(c0 + lane_pos <= tpos, score, -jnp.inf)
        bits = lax.bitcast_convert_type(score, i32)
        key_ref[c] = bits ^ ((bits >> 31) & 0x7FFFFFFF)
        return carry

    lax.fori_loop(0, nck, score_body, 0)

    def bit_body(b, p):
        cand_u = p | lax.shift_left(jnp.int32(1), 31 - b)
        cand = cand_u ^ INT_MIN
        cnt = count(lambda k, kp: k >= cand)
        return jnp.where(cnt >= topk, cand_u, p)

    tau = lax.fori_loop(0, 32, bit_body, jnp.zeros((TQ, 1), i32)) ^ INT_MIN

    cnt_gt = count(lambda k, kp: k > tau)
    cnt_ge = count(lambda k, kp: k >= tau)
    need = (cnt_ge > topk) & (tau > NEG_INF_KEY)
    quota = topk - cnt_gt
    j_ref[...] = jnp.full((TQ, 128), S, i32)

    @pl.when(jnp.max(jnp.where(need, 1.0, 0.0)) > 0.0)
    def _():
        nbits = max(1, (S - 1).bit_length())

        def jbit(b, lo):
            cand = lo | lax.shift_left(jnp.int32(1), nbits - 1 - b)
            c = count(lambda k, kp: (k == tau) & (kp < cand))
            return jnp.where(c < quota, cand, lo)

        jstar = lax.fori_loop(0, nbits, jbit, jnp.zeros((TQ, 1), i32))
        j_ref[...] = jnp.broadcast_to(jnp.where(need, jstar, S), (TQ, 128))

    jlim = j_ref[:, 0:1]

    def bias_body(c, carry):
        k = key_ref[c]
        kp = c * ck + lane_pos
        sel = ((k > tau) | ((k == tau) & (kp <= jlim))) & (kp <= tpos)
        bias_ref[c] = jnp.where(sel, 0.0, NEG)
        return carry

    lax.fori_loop(0, nck, bias_body, 0)

    q = q_ref[0]
    outs = []
    for h in range(H_A):
        hs = slice(h * HEAD_DIM, (h + 1) * HEAD_DIM)
        qh = q[:, hs]
        slope = slopes[h]

        def att_body(c, carry, hs=hs, qh=qh, slope=slope):
            m, l, acc = carry
            c0 = pl.multiple_of(c * ck, ck)
            kc = k_ref[0, pl.ds(c0, ck), :][:, hs]
            vc = v_ref[0, pl.ds(c0, ck), :][:, hs]
            s = lax.dot_general(qh, kc, _NT, preferred_element_type=f32)
            dist = (tpos - (c0 + lane_pos)).astype(f32)
            logit = s - slope * dist + bias_ref[c]
            m_new = jnp.maximum(m, jnp.max(logit, axis=-1, keepdims=True))
            alpha = jnp.exp(m - m_new)
            p = jnp.exp(logit - m_new)
            l = alpha * l + jnp.sum(p, axis=-1, keepdims=True)
            acc = alpha * acc + jnp.dot(p.astype(bf16), vc, preferred_element_type=f32)
            return m_new, l, acc

        m, l, acc = lax.fori_loop(0, nck, att_body,
                                  (jnp.full((TQ, 1), NEG, f32), jnp.zeros((TQ, 1), f32),
                                   jnp.zeros((TQ, HEAD_DIM), f32)))
        outs.append(acc / l)
    o_ref[0] = jnp.concatenate(outs, axis=-1)


def _dsa(qkv, qi, kib, kw, *, topk, slopes, ck):
    B, S, _ = qkv.shape
    W = H_A * HEAD_DIM
    WI = IDX_HEADS * IDX_DIM
    return pl.pallas_call(
        functools.partial(_dsa_kernel, topk=topk, slopes=slopes, ck=ck),
        out_shape=jax.ShapeDtypeStruct((B, S, W), f32),
        grid=(B, S // TQ),
        in_specs=[pl.BlockSpec((1, TQ, W), lambda b, i: (b, i, 0)),
                  pl.BlockSpec((1, S, W), lambda b, i: (b, 0, 1)),
                  pl.BlockSpec((1, S, W), lambda b, i: (b, 0, 2)),
                  pl.BlockSpec((1, TQ, WI), lambda b, i: (b, i, 0)),
                  pl.BlockSpec((1, S, 128), lambda b, i: (b, 0, 0)),
                  pl.BlockSpec((1, TQ, 128), lambda b, i: (b, i, 0))],
        out_specs=pl.BlockSpec((1, TQ, W), lambda b, i: (b, i, 0)),
        scratch_shapes=[pltpu.VMEM((S // ck, TQ, ck), i32),
                        pltpu.VMEM((S // ck, TQ, ck), f32),
                        pltpu.VMEM((TQ, 128), i32)],
        compiler_params=_params(("parallel", "arbitrary"), 40),
        name="dsa_attention",
    )(qkv, qkv, qkv, qi, kib, kw)


def _sb_kernel(q_ref, k_ref, v_ref, o_ref, *, kb):
    i = pl.program_id(1)
    t0 = i * TQ
    nkb = lax.div(t0 + TQ + kb - 1, kb)
    tpos = t0 + lax.broadcasted_iota(i32, (TQ, 1), 0)
    lane_pos = lax.broadcasted_iota(i32, (1, kb), 1)
    tri = jnp.where(lax.broadcasted_iota(i32, (kb, kb), 0) > lax.broadcasted_iota(i32, (kb, kb), 1),
                    1.0, 0.0).astype(bf16)
    q = q_ref[0]
    outs = []
    for h in range(H_B):
        hs = slice(h * HEAD_DIM, (h + 1) * HEAD_DIM)
        qh = q[:, hs]

        def body(jj, carry, hs=hs, qh=qh):
            run, acc = carry
            s0 = pl.multiple_of((nkb - 1 - jj) * kb, kb)
            kc = k_ref[0, pl.ds(s0, kb), :][:, hs]
            vc = v_ref[0, pl.ds(s0, kb), :][:, hs]
            z = lax.dot_general(qh, kc, _NT, preferred_element_type=f32)
            strict = (s0 + lane_pos) < tpos
            sp = jnp.maximum(z, 0.0) + jnp.log1p(jnp.exp(-jnp.abs(z)))
            lm = jnp.where(strict, -sp, 0.0)
            l1 = lm.astype(bf16)
            r1 = lm - l1.astype(f32)
            l2 = r1.astype(bf16)
            l3 = (r1 - l2.astype(f32)).astype(bf16)
            later = (jnp.dot(l1, tri, preferred_element_type=f32)
                     + jnp.dot(l2, tri, preferred_element_type=f32)
                     + jnp.dot(l3, tri, preferred_element_type=f32))
            a = jnp.where(strict, jnp.exp((z - sp) + (run + later)), 0.0)
            acc = acc + jnp.dot(a.astype(bf16), vc, preferred_element_type=f32)
            run = run + jnp.sum(lm, axis=-1, keepdims=True)
            return run, acc

        _, acc = lax.fori_loop(0, nkb, body, (jnp.zeros((TQ, 1), f32), jnp.zeros((TQ, HEAD_DIM), f32)))
        outs.append(acc)
    o_ref[0] = jnp.concatenate(outs, axis=-1)


def _stick_breaking(qkv, *, kb):
    B, S, _ = qkv.shape
    W = H_B * HEAD_DIM
    return pl.pallas_call(
        functools.partial(_sb_kernel, kb=kb),
        out_shape=jax.ShapeDtypeStruct((B, S, W), f32),
        grid=(B, S // TQ),
        in_specs=[pl.BlockSpec((1, TQ, W), lambda b, i: (b, i, 0)),
                  pl.BlockSpec((1, S, W), lambda b, i: (b, 0, 1)),
                  pl.BlockSpec((1, S, W), lambda b, i: (b, 0, 2))],
        out_specs=pl.BlockSpec((1, TQ, W), lambda b, i: (b, i, 0)),
        compiler_params=_params(("parallel", "arbitrary"), 32),
        name="stick_breaking",
    )(qkv, qkv, qkv)


def _dil_kernel(*refs, dil, slopes, first, last):
    if first:
        q_ref, kp_ref, kc_ref, vp_ref, vc_ref, acc_out, st_out = refs
    elif last:
        q_ref, kp_ref, kc_ref, vp_ref, vc_ref, acc_in, st_in, o_ref = refs
    else:
        q_ref, kp_ref, kc_ref, vp_ref, vc_ref, acc_in, st_in, acc_out, st_out = refs
    j = pl.program_id(2)
    u = lax.broadcasted_iota(i32, (TQ, 1), 0)
    c = lax.broadcasted_iota(i32, (1, TQ), 1)
    d_cur = u - c
    d_prev = d_cur + TQ
    ok_cur = d_cur >= 0
    ok_prev = (d_prev <= TQ) & (j > 0)
    dist_cur = d_cur.astype(f32) * float(dil)
    dist_prev = d_prev.astype(f32) * float(dil)
    q = q_ref[0]
    kp, kc, vp, vc = kp_ref[0], kc_ref[0], vp_ref[0], vc_ref[0]
    if not first:
        st = st_in[0]
        acc_prev = acc_in[0]
    lane = lax.broadcasted_iota(i32, (TQ, 128), 1)
    st_new = jnp.zeros((TQ, 128), f32)
    accs = []
    for h in range(H_C):
        hs = slice(h * HEAD_DIM, (h + 1) * HEAD_DIM)
        qh = q[:, hs]
        s_p = lax.dot_general(qh, kp[:, hs], _NT, preferred_element_type=f32)
        s_c = lax.dot_general(qh, kc[:, hs], _NT, preferred_element_type=f32)
        lg_p = jnp.where(ok_prev, s_p - slopes[h] * dist_prev, NEG)
        lg_c = jnp.where(ok_cur, s_c - slopes[h] * dist_cur, NEG)
        m_new = jnp.maximum(jnp.max(lg_p, axis=-1, keepdims=True), jnp.max(lg_c, axis=-1, keepdims=True))
        if not first:
            m_old = st[:, h:h + 1]
            l_old = st[:, H_C + h:H_C + h + 1]
            m_new = jnp.maximum(m_new, m_old)
            alpha = jnp.exp(m_old - m_new)
        p_p = jnp.exp(lg_p - m_new)
        p_c = jnp.exp(lg_c - m_new)
        l_new = jnp.sum(p_p, axis=-1, keepdims=True) + jnp.sum(p_c, axis=-1, keepdims=True)
        acc = (jnp.dot(p_p.astype(bf16), vp[:, hs], preferred_element_type=f32)
               + jnp.dot(p_c.astype(bf16), vc[:, hs], preferred_element_type=f32))
        if not first:
            l_new = l_new + alpha * l_old
            acc = acc + alpha * acc_prev[:, hs]
        if last:
            accs.append(acc / l_new)
        else:
            accs.append(acc)
            st_new = jnp.where(lane == h, m_new, st_new)
            st_new = jnp.where(lane == H_C + h, l_new, st_new)
    if last:
        o_ref[0] = jnp.concatenate(accs, axis=-1)
    else:
        acc_out[0] = jnp.concatenate(accs, axis=-1)
        st_out[0] = st_new


def _dilated_branch(qkv, acc, st, *, dil, slopes, first, last):
    B, S, W3 = qkv.shape
    W = W3 // 3
    Lc = S // dil
    nj = Lc // TQ
    qkv_v = qkv.reshape(B, Lc, dil * W3)
    blk = lambda comp, prev: pl.BlockSpec(
        (1, TQ, W), (lambda b, r, j: (b, jnp.maximum(j - 1, 0), r * 3 + comp)) if prev
        else (lambda b, r, j: (b, j, r * 3 + comp)))
    acc_spec = pl.BlockSpec((1, TQ, W), lambda b, r, j: (b, j, r))
    st_spec = pl.BlockSpec((1, TQ, 128), lambda b, r, j: (b, j, r))
    in_specs = [blk(0, False), blk(1, True), blk(1, False), blk(2, True), blk(2, False)]
    args = [qkv_v] * 5
    if not first:
        in_specs += [acc_spec, st_spec]
        args += [acc.reshape(B, Lc, dil * W), st.reshape(B, Lc, dil * 128)]
    if last:
        out_shape = jax.ShapeDtypeStruct((B, Lc, dil * W), f32)
        out_specs = acc_spec
    else:
        out_shape = [jax.ShapeDtypeStruct((B, Lc, dil * W), f32), jax.ShapeDtypeStruct((B, Lc, dil * 128), f32)]
        out_specs = [acc_spec, st_spec]
    res = pl.pallas_call(
        functools.partial(_dil_kernel, dil=dil, slopes=slopes, first=first, last=last),
        out_shape=out_shape,
        grid=(B, dil, nj),
        in_specs=in_specs,
        out_specs=out_specs,
        compiler_params=_params(("parallel", "parallel", "arbitrary"), 32),
        name=f"dilated_d{dil}",
    )(*args)
    if last:
        return res.reshape(B, S, W)
    return res[0].reshape(B, S, W), res[1].reshape(B, S, 128)


def _dilated(qkv, slopes):
    order = sorted(DILATIONS, key=lambda wd: -wd[1])
    acc = st = None
    for n, (window, dil) in enumerate(order):
        assert window // dil == TQ
        first, last = n == 0, n == len(order) - 1
        res = _dilated_branch(qkv, acc, st, dil=dil, slopes=slopes, first=first, last=last)
        if last:
            return res
        acc, st = res


def _out_kernel(oa_ref, ob_ref, oc_ref, x_ref, gt_ref, go_ref, w_ref, o_ref):
    go = go_ref[...]
    y = None
    o = 0
    for ref in (oa_ref, ob_ref, oc_ref):
        v = ref[0]
        w = v.shape[1]
        ms = jnp.mean(v * v, axis=-1, keepdims=True)
        yn = (v * lax.rsqrt(ms + EPS) * go[:, o:o + w]).astype(bf16)
        part = jnp.dot(yn, w_ref[o:o + w, :], preferred_element_type=f32)
        y = part if y is None else y + part
        o += w
    o_ref[0] = x_ref[0] + gt_ref[0] * y


def _out_proj(oa, ob, oc, x, gt, go, w, *, tm):
    B, S, D = x.shape
    blk = lambda a: pl.BlockSpec((1, tm, a.shape[2]), lambda b, j: (b, j, 0))
    return pl.pallas_call(
        _out_kernel,
        out_shape=jax.ShapeDtypeStruct((B, S, D), f32),
        grid=(B, S // tm),
        in_specs=[blk(oa), blk(ob), blk(oc), blk(x), pl.BlockSpec((1, 1, D), lambda b, j: (b, 0, 0)),
                  _resident(go.shape), _resident(w.shape)],
        out_specs=blk(x),
        compiler_params=_params(("parallel", "parallel"), 40),
        name="mixer_out_proj",
    )(oa, ob, oc, x, gt, go, w)


def _regroup_w_in(w_in):
    WA, WB, WC = H_A * HEAD_DIM, H_B * HEAD_DIM, H_C * HEAD_DIM
    WI = IDX_HEADS * IDX_DIM
    sizes = [WA] * 3 + [WI, IDX_DIM, IDX_HEADS] + [WB] * 3 + [WC] * 3
    offs = np.concatenate([[0], np.cumsum(sizes)])
    seg = [w_in[:, offs[n]:offs[n + 1]] for n in range(len(sizes))]
    qa, ka, va, qi, ki, wi, qb, kb, vb, qc, kc, vc = seg
    pad = jnp.zeros((w_in.shape[0], 128 - IDX_DIM - IDX_HEADS), w_in.dtype)
    return jnp.concatenate([qa, ka, va, qb, kb, vb, qc, kc, vc, qi, ki, wi, pad], axis=1).astype(bf16)


def kernel(x, c, w_ada, b_ada, norm_ffn1, w_gate1, w_up1, w_down1, norm_mix, w_in, qn_a, kn_a, qn_c, kn_c,
           g_out, w_out, norm_ffn2, w_gate2, w_up2, w_down2):
    B, S, D = x.shape
    L = w_ada.shape[0]
    topk = min(TOPK_MAX, S // 4)
    slopes_a, slopes_c = _alibi_slopes()
    tm = min(512, S)
    mod = _modulation(c, w_ada, b_ada).reshape(L, B, N_MOD, 1, D)
    tile = lambda g, n: jnp.tile(g, n).reshape(1, -1)
    for l in range(L):
        sh1, sc1, g1, sh2, sc2, g2, sh3, sc3, g3 = [mod[l, :, n] for n in range(N_MOD)]
        x = _ffn(x, sh1, sc1, g1, norm_ffn1[l].reshape(1, D), w_gate1[l].astype(bf16), w_up1[l].astype(bf16),
                 w_down1[l].astype(bf16), tm=tm, n_chunks=2)
        qkv_a, qkv_b, qkv_c, qi, kib, kw = _proj(
            x, sh2, sc2, norm_mix[l].reshape(1, D), _regroup_w_in(w_in[l]),
            tile(qn_a[l], H_A), tile(kn_a[l], H_A), tile(qn_c[l], H_C), tile(kn_c[l], H_C), tm=tm)
        o_a = _dsa(qkv_a, qi, kib, kw, topk=topk, slopes=slopes_a, ck=min(512, S))
        o_b = _stick_breaking(qkv_b, kb=256)
        o_c = _dilated(qkv_c, slopes_c)
        x = _out_proj(o_a, o_b, o_c, x, g2, g_out[l].reshape(1, -1), w_out[l].astype(bf16), tm=tm)
        x = _ffn(x, sh3, sc3, g3, norm_ffn2[l].reshape(1, D), w_gate2[l].astype(bf16), w_up2[l].astype(bf16),
                 w_down2[l].astype(bf16), tm=tm, n_chunks=2)
    return x
```

```python
import functools

import numpy as np
import jax
import jax.numpy as jnp
from jax import lax
from jax.experimental import pallas as pl
from jax.experimental.pallas import tpu as pltpu

HEAD_DIM = 64
H_A = 4
H_B = 4
H_C = 8
IDX_HEADS = 8
IDX_DIM = 64
TOPK_MAX = 256
DILATIONS = ((128, 1), (512, 4), (2048, 16))
N_MOD = 9
EPS = 1e-6

TQ = 128
LANES = 128
NEG = -1e30
INT_MIN = -2 ** 31
NEG_INF_KEY = -2139095041
EXP_DEAD = -104.0
V7X_VMEM_BYTES = 64 * 1024 * 1024

f32 = jnp.float32
bf16 = jnp.bfloat16
i32 = jnp.int32

_NT = (((1,), (1,)), ((), ()))


def _params(sem, vmem_mb):
    return pltpu.CompilerParams(dimension_semantics=sem,
                                vmem_limit_bytes=min(vmem_mb * 1024 * 1024, V7X_VMEM_BYTES - (4 << 20)))


def _alibi_slopes():
    n = H_A + H_C
    s = np.exp2(-8.0 * np.arange(1, n + 1, dtype=np.float32) / np.float32(n)).astype(np.float32)
    idx = np.arange(n)
    return [float(v) for v in s[idx % 3 == 2]], [float(v) for v in s[idx % 3 != 2]]


def _resident(shape):
    nd = len(shape)
    return pl.BlockSpec(shape, lambda *_: (0,) * nd, pipeline_mode=pl.Buffered(1))


def _mod_kernel(c_ref, w_ref, b_ref, o_ref):
    c = c_ref[...]
    cond = c * jax.nn.sigmoid(c)
    o_ref[0] = jnp.dot(cond.astype(bf16), w_ref[0].astype(bf16), preferred_element_type=f32) + b_ref[0]


def _modulation(c, w_ada, b_ada):
    L, D, ND = w_ada.shape
    B = c.shape[0]
    tn = D
    return pl.pallas_call(
        _mod_kernel,
        out_shape=jax.ShapeDtypeStruct((L, B, ND), f32),
        grid=(L, ND // tn),
        in_specs=[pl.BlockSpec((B, D), lambda l, n: (0, 0)),
                  pl.BlockSpec((1, D, tn), lambda l, n: (l, 0, n)),
                  pl.BlockSpec((1, 1, tn), lambda l, n: (l, 0, n))],
        out_specs=pl.BlockSpec((1, B, tn), lambda l, n: (l, 0, n)),
        compiler_params=_params(("arbitrary", "arbitrary"), 32),
        name="adaln_mod",
    )(c, w_ada, b_ada.reshape(L, 1, ND))


def _norm_mod(x, nw, sc, sh):
    ms = jnp.mean(x * x, axis=-1, keepdims=True)
    y = x * lax.rsqrt(ms + EPS) * nw
    return y * (1.0 + sc) + sh


def _ffn_kernel(x_ref, sh_ref, sc_ref, gt_ref, nw_ref, wg_ref, wu_ref, wd_ref, o_ref, *, n_chunks):
    x = x_ref[0]
    h = _norm_mod(x, nw_ref[...], sc_ref[0], sh_ref[0]).astype(bf16)
    F = wg_ref.shape[1]
    fc = F // n_chunks
    y = None
    for c in range(n_chunks):
        g = jnp.dot(h, wg_ref[:, c * fc:(c + 1) * fc], preferred_element_type=f32)
        u = jnp.dot(h, wu_ref[:, c * fc:(c + 1) * fc], preferred_element_type=f32)
        a = (g * jax.nn.sigmoid(g) * u).astype(bf16)
        part = jnp.dot(a, wd_ref[c * fc:(c + 1) * fc, :], preferred_element_type=f32)
        y = part if y is None else y + part
    o_ref[0] = x + (0.5 * gt_ref[0]) * y


def _ffn(x, sh, sc, gt, nw, wg, wu, wd, *, tm, n_chunks):
    B, S, D = x.shape
    F = wg.shape[1]
    vec = pl.BlockSpec((1, 1, D), lambda b, j: (b, 0, 0))
    return pl.pallas_call(
        functools.partial(_ffn_kernel, n_chunks=n_chunks),
        out_shape=jax.ShapeDtypeStruct((B, S, D), f32),
        grid=(B, S // tm),
        in_specs=[pl.BlockSpec((1, tm, D), lambda b, j: (b, j, 0)), vec, vec, vec,
                  _resident((1, D)), _resident((D, F)), _resident((D, F)), _resident((F, D))],
        out_specs=pl.BlockSpec((1, tm, D), lambda b, j: (b, j, 0)),
        compiler_params=_params(("parallel", "parallel"), 56),
        name="ffn_swiglu",
    )(x, sh, sc, gt, nw, wg, wu, wd)


def _head_rms(x, gain_row, scale):
    W = x.shape[1]
    outs = []
    r = lax.broadcasted_iota(i32, (256, 256), 0) >> 6
    c = lax.broadcasted_iota(i32, (256, 256), 1) >> 6
    bd = jnp.where(r == c, 1.0, 0.0).astype(bf16)
    for s in range(W // 256):
        xs = x[:, s * 256:(s + 1) * 256]
        sq = xs * xs
        hi = sq.astype(bf16)
        lo = (sq - hi.astype(f32)).astype(bf16)
        ss = jnp.dot(hi, bd, preferred_element_type=f32) + jnp.dot(lo, bd, preferred_element_type=f32)
        y = xs * lax.rsqrt(ss * (1.0 / HEAD_DIM) + EPS) * gain_row[:, s * 256:(s + 1) * 256]
        outs.append(y * scale if scale != 1.0 else y)
    return outs[0] if len(outs) == 1 else jnp.concatenate(outs, axis=-1)


def _proj_kernel(x_ref, sh_ref, sc_ref, nw_ref, w_ref, qna_ref, kna_ref, qnc_ref, knc_ref,
                 a_ref, b_ref, c_ref, qi_ref, kib_ref, kw_ref):
    x = x_ref[0]
    h = _norm_mod(x, nw_ref[...], sc_ref[0], sh_ref[0]).astype(bf16)
    WA, WB, WC = H_A * HEAD_DIM, H_B * HEAD_DIM, H_C * HEAD_DIM
    qscale = HEAD_DIM ** -0.5
    o = 0

    def mm(width):
        nonlocal o
        r = jnp.dot(h, w_ref[:, o:o + width], preferred_element_type=f32)
        o += width
        return r

    pa = mm(3 * WA)
    a_ref[0] = jnp.concatenate([_head_rms(pa[:, :WA], qna_ref[...], qscale),
                                _head_rms(pa[:, WA:2 * WA], kna_ref[...], 1.0),
                                pa[:, 2 * WA:]], axis=-1).astype(bf16)
    pb = mm(3 * WB)
    b_ref[0] = jnp.concatenate([pb[:, :WB] * qscale, pb[:, WB:]], axis=-1).astype(bf16)
    pc = mm(3 * WC)
    c_ref[0] = jnp.concatenate([_head_rms(pc[:, :WC], qnc_ref[...], qscale),
                                _head_rms(pc[:, WC:2 * WC], knc_ref[...], 1.0),
                                pc[:, 2 * WC:]], axis=-1).astype(bf16)
    qi_ref[0] = mm(IDX_HEADS * IDX_DIM).astype(bf16)
    kw = mm(LANES)
    kib_ref[0] = kw.astype(bf16)
    kw_ref[0] = kw


def _proj(x, sh, sc, nw, w2, qna, kna, qnc, knc, *, tm):
    B, S, D = x.shape
    WA, WB, WC = H_A * HEAD_DIM, H_B * HEAD_DIM, H_C * HEAD_DIM
    WI = IDX_HEADS * IDX_DIM
    vec = pl.BlockSpec((1, 1, D), lambda b, j: (b, 0, 0))
    widths = (3 * WA, 3 * WB, 3 * WC, WI, LANES, LANES)
    dtypes = (bf16, bf16, bf16, bf16, bf16, f32)
    return pl.pallas_call(
        _proj_kernel,
        out_shape=[jax.ShapeDtypeStruct((B, S, w), dt) for w, dt in zip(widths, dtypes)],
        grid=(B, S // tm),
        in_specs=[pl.BlockSpec((1, tm, D), lambda b, j: (b, j, 0)), vec, vec,
                  _resident((1, D)), _resident(w2.shape),
                  _resident((1, WA)), _resident((1, WA)), _resident((1, WC)), _resident((1, WC))],
        out_specs=[pl.BlockSpec((1, tm, w), lambda b, j: (b, j, 0)) for w in widths],
        compiler_params=_params(("parallel", "parallel"), 48),
        name="mixer_in_proj",
    )(x, sh, sc, nw, w2, qna, kna, qnc, knc)


def _dsa_kernel(q_ref, k_ref, v_ref, qi_ref, ki_ref, w_ref, o_ref,
                key_ref, bias_ref, logit_ref, mx_ref, sum_ref, acc_ref, *, topk, slopes, ck):
    S = k_ref.shape[1]
    i = pl.program_id(1)
    t0 = i * TQ
    nck = lax.div(t0 + TQ + ck - 1, ck)
    lt = ck // LANES
    tpos = t0 + lax.broadcasted_iota(i32, (TQ, 1), 0)
    lane_pos = lax.broadcasted_iota(i32, (1, ck), 1)
    idx_scale = (IDX_DIM ** -0.5) * (IDX_HEADS ** -0.5)

    def fold(m):
        r = m[:, 0:LANES]
        for t in range(1, lt):
            r = r + m[:, t * LANES:(t + 1) * LANES]
        return r

    def count(pred_fn):
        def body(c, acc):
            m = pred_fn(key_ref[c], c * ck + lane_pos)
            return acc + fold(jnp.where(m, 1, 0).astype(i32))
        acc = lax.fori_loop(0, nck, body, jnp.zeros((TQ, LANES), i32))
        return jnp.sum(acc.astype(f32), axis=-1, keepdims=True)

    qi = qi_ref[0]
    qstack = jnp.concatenate([qi[:, h * IDX_DIM:(h + 1) * IDX_DIM] for h in range(IDX_HEADS)], axis=0)
    wts = w_ref[0]
    wcols = [wts[:, IDX_DIM + h:IDX_DIM + h + 1] for h in range(IDX_HEADS)]

    def score_body(c, carry):
        c0 = pl.multiple_of(c * ck, ck)
        kic = ki_ref[0, pl.ds(c0, ck), :][:, :IDX_DIM]
        rel = lax.dot_general(qstack, kic, _NT, preferred_element_type=f32)
        score = wcols[0] * jnp.maximum(rel[0:TQ], 0.0)
        for h in range(1, IDX_HEADS):
            score = score + wcols[h] * jnp.maximum(rel[h * TQ:(h + 1) * TQ], 0.0)
        score = score * idx_scale
        score = jnp.where(score == 0.0, 0.0, score)
        score = jnp.where(c0 + lane_pos <= tpos, score, -jnp.inf)
        bits = lax.bitcast_convert_type(score, i32)
        key_ref[c] = bits ^ ((bits >> 31) & 0x7FFFFFFF)
        return carry

    lax.fori_loop(0, nck, score_body, 0)

    def bit_body(b, p):
        cand_u = p | lax.shift_left(jnp.int32(1), 31 - b)
        cand = cand_u ^ INT_MIN
        cnt = count(lambda k, kp: k >= cand)
        return jnp.where(cnt >= topk, cand_u, p)

    tau = lax.fori_loop(0, 32, bit_body, jnp.zeros((TQ, 1), i32)) ^ INT_MIN

    cnt_gt = count(lambda k, kp: k > tau)
    cnt_ge = count(lambda k, kp: k >= tau)
    need = (cnt_ge > topk) & (tau > NEG_INF_KEY)
    any_need = jnp.max(jnp.where(need, 1.0, 0.0)) > 0.0

    @pl.when(jnp.logical_not(any_need))
    def _():
        tau_eff = jnp.maximum(tau, NEG_INF_KEY + 1)

        def bias_body(c, carry):
            bias_ref[c] = jnp.where(key_ref[c] >= tau_eff, 0.0, NEG)
            return carry

        lax.fori_loop(0, nck, bias_body, 0)

    @pl.when(any_need)
    def _():
        quota = topk - cnt_gt
        nbits = max(1, (S - 1).bit_length())

        def jbit(b, lo):
            cand = lo | lax.shift_left(jnp.int32(1), nbits - 1 - b)
            c = count(lambda k, kp: (k == tau) & (kp < cand))
            return jnp.where(c < quota, cand, lo)

        jstar = lax.fori_loop(0, nbits, jbit, jnp.zeros((TQ, 1), i32))
        jlim = jnp.where(need, jstar, S)

        def bias_body(c, carry):
            k = key_ref[c]
            kp = c * ck + lane_pos
            sel = ((k > tau) | ((k == tau) & (kp <= jlim))) & (kp <= tpos)
            bias_ref[c] = jnp.where(sel, 0.0, NEG)
            return carry

        lax.fori_loop(0, nck, bias_body, 0)

    lane = lax.broadcasted_iota(i32, (1, LANES), 1)
    own = [(lane < HEAD_DIM) if h % 2 == 0 else (lane >= HEAD_DIM) for h in range(H_A)]
    qm = []
    for h in range(H_A):
        qt = q_ref[0, :, (h // 2) * LANES:(h // 2 + 1) * LANES]
        qm.append(jnp.where(own[h], qt, jnp.zeros_like(qt)))

    def fold_max(x):
        r = x[:, 0:LANES]
        for t in range(1, lt):
            r = jnp.maximum(r, x[:, t * LANES:(t + 1) * LANES])
        return r

    mx_ref[...] = jnp.full(mx_ref.shape, NEG, f32)

    def logit_body(c, carry):
        c0 = pl.multiple_of(c * ck, ck)
        bias = bias_ref[c]
        pos = (c0 + lane_pos).astype(f32)
        for h in range(H_A):
            kt = k_ref[0, pl.ds(c0, ck), (h // 2) * LANES:(h // 2 + 1) * LANES]
            lg = lax.dot_general(qm[h], kt, _NT, preferred_element_type=f32) + (slopes[h] * pos + bias)
            logit_ref[h, c] = lg
            mx_ref[h] = jnp.maximum(mx_ref[h], fold_max(lg))
        return carry

    lax.fori_loop(0, nck, logit_body, 0)
    row_max = [jnp.max(mx_ref[h], axis=-1, keepdims=True) for h in range(H_A)]
    sum_ref[...] = jnp.zeros(sum_ref.shape, f32)
    acc_ref[...] = jnp.zeros(acc_ref.shape, f32)

    def pv_body(c, carry):
        c0 = pl.multiple_of(c * ck, ck)
        for h in range(H_A):
            g = h // 2
            vt = v_ref[0, pl.ds(c0, ck), g * LANES:(g + 1) * LANES]
            p = jnp.exp(logit_ref[h, c] - row_max[h])
            sum_ref[h] = sum_ref[h] + fold(p)
            acc_ref[g] = acc_ref[g] + jnp.dot(p.astype(bf16), jnp.where(own[h], vt, jnp.zeros_like(vt)),
                                              preferred_element_type=f32)
        return carry

    lax.fori_loop(0, nck, pv_body, 0)
    outs = []
    for g in range(H_A // 2):
        l_e = jnp.sum(sum_ref[2 * g], axis=-1, keepdims=True)
        l_o = jnp.sum(sum_ref[2 * g + 1], axis=-1, keepdims=True)
        outs.append(acc_ref[g] / jnp.where(own[2 * g], l_e, l_o))
    o_ref[0] = jnp.concatenate(outs, axis=-1)


def _dsa(qkv, qi, kib, kw, *, topk, slopes, ck):
    B, S, _ = qkv.shape
    W = H_A * HEAD_DIM
    WI = IDX_HEADS * IDX_DIM
    return pl.pallas_call(
        functools.partial(_dsa_kernel, topk=topk, slopes=slopes, ck=ck),
        out_shape=jax.ShapeDtypeStruct((B, S, W), f32),
        grid=(B, S // TQ),
        in_specs=[pl.BlockSpec((1, TQ, W), lambda b, i: (b, i, 0)),
                  pl.BlockSpec((1, S, W), lambda b, i: (b, 0, 1)),
                  pl.BlockSpec((1, S, W), lambda b, i: (b, 0, 2)),
                  pl.BlockSpec((1, TQ, WI), lambda b, i: (b, i, 0)),
                  pl.BlockSpec((1, S, LANES), lambda b, i: (b, 0, 0)),
                  pl.BlockSpec((1, TQ, LANES), lambda b, i: (b, i, 0))],
        out_specs=pl.BlockSpec((1, TQ, W), lambda b, i: (b, i, 0)),
        scratch_shapes=[pltpu.VMEM((S // ck, TQ, ck), i32),
                        pltpu.VMEM((S // ck, TQ, ck), f32),
                        pltpu.VMEM((H_A, S // ck, TQ, ck), f32),
                        pltpu.VMEM((H_A, TQ, LANES), f32),
                        pltpu.VMEM((H_A, TQ, LANES), f32),
                        pltpu.VMEM((H_A // 2, TQ, LANES), f32)],
        compiler_params=_params(("parallel", "arbitrary"), 48),
        name="dsa_attention",
    )(qkv, qkv, qkv, qi, kib, kw)


def _sb_kernel(q_ref, k_ref, v_ref, o_ref, *, kb):
    i = pl.program_id(1)
    t0 = i * TQ
    nkb = lax.div(t0 + TQ + kb - 1, kb)
    tpos = t0 + lax.broadcasted_iota(i32, (TQ, 1), 0)
    lane_pos = lax.broadcasted_iota(i32, (1, kb), 1)
    tri = jnp.where(lax.broadcasted_iota(i32, (kb, kb), 0) > lax.broadcasted_iota(i32, (kb, kb), 1),
                    1.0, 0.0).astype(bf16)
    q = q_ref[0]
    qs = [q[:, h * HEAD_DIM:(h + 1) * HEAD_DIM] for h in range(H_B)]

    def cond(carry):
        jj, alive = carry[0], carry[1]
        return (jj < nkb) & (alive > EXP_DEAD)

    def body(carry):
        jj, _, runs, accs = carry
        s0 = pl.multiple_of((nkb - 1 - jj) * kb, kb)
        kt = k_ref[0, pl.ds(s0, kb), :]
        vt = v_ref[0, pl.ds(s0, kb), :]
        strict = (s0 + lane_pos) < tpos
        zs, sps, lms, parts = [], [], [], ([], [], [])
        for h in range(H_B):
            hs = slice(h * HEAD_DIM, (h + 1) * HEAD_DIM)
            z = lax.dot_general(qs[h], kt[:, hs], _NT, preferred_element_type=f32)
            sp = jnp.maximum(z, 0.0) + jnp.log1p(jnp.exp(-jnp.abs(z)))
            lm = jnp.where(strict, -sp, 0.0)
            l1 = lm.astype(bf16)
            r1 = lm - l1.astype(f32)
            l2 = r1.astype(bf16)
            l3 = (r1 - l2.astype(f32)).astype(bf16)
            zs.append(z); sps.append(sp); lms.append(lm)
            parts[0].append(l1); parts[1].append(l2); parts[2].append(l3)
        later = sum(jnp.dot(jnp.concatenate(p, axis=0), tri, preferred_element_type=f32) for p in parts)
        new_runs, new_accs = [], []
        alive = None
        for h in range(H_B):
            hs = slice(h * HEAD_DIM, (h + 1) * HEAD_DIM)
            a = jnp.where(strict, jnp.exp((zs[h] - sps[h]) + (runs[h] + later[h * TQ:(h + 1) * TQ])), 0.0)
            new_accs.append(accs[h] + jnp.dot(a.astype(bf16), vt[:, hs], preferred_element_type=f32))
            run = runs[h] + jnp.sum(lms[h], axis=-1, keepdims=True)
            new_runs.append(run)
            alive = run if alive is None else jnp.maximum(alive, run)
        return jj + 1, jnp.max(alive), tuple(new_runs), tuple(new_accs)

    init = (jnp.int32(0), jnp.float32(0.0),
            tuple(jnp.zeros((TQ, 1), f32) for _ in range(H_B)),
            tuple(jnp.zeros((TQ, HEAD_DIM), f32) for _ in range(H_B)))
    _, _, _, accs = lax.while_loop(cond, body, init)
    o_ref[0] = jnp.concatenate(accs, axis=-1)


def _stick_breaking(qkv, *, kb):
    B, S, _ = qkv.shape
    W = H_B * HEAD_DIM
    return pl.pallas_call(
        functools.partial(_sb_kernel, kb=kb),
        out_shape=jax.ShapeDtypeStruct((B, S, W), f32),
        grid=(B, S // TQ),
        in_specs=[pl.BlockSpec((1, TQ, W), lambda b, i: (b, i, 0)),
                  pl.BlockSpec((1, S, W), lambda b, i: (b, 0, 1)),
                  pl.BlockSpec((1, S, W), lambda b, i: (b, 0, 2))],
        out_specs=pl.BlockSpec((1, TQ, W), lambda b, i: (b, i, 0)),
        compiler_params=_params(("parallel", "arbitrary"), 32),
        name="stick_breaking",
    )(qkv, qkv, qkv)


def _dil_kernel(*refs, dil, slopes, first, last):
    if first:
        q_ref, kp_ref, kc_ref, vp_ref, vc_ref, acc_out, st_out, bias_ref = refs
    elif last:
        q_ref, kp_ref, kc_ref, vp_ref, vc_ref, acc_in, st_in, o_ref, bias_ref = refs
    else:
        q_ref, kp_ref, kc_ref, vp_ref, vc_ref, acc_in, st_in, acc_out, st_out, bias_ref = refs
    j = pl.program_id(2)

    @pl.when((pl.program_id(0) == 0) & (pl.program_id(1) == 0) & (j == 0))
    def _():
        u = lax.broadcasted_iota(i32, (TQ, 1), 0)
        c = lax.broadcasted_iota(i32, (1, TQ), 1)
        d_cur = u - c
        d_prev = d_cur + TQ
        for h in range(H_C):
            bias_ref[h, 0] = jnp.where(d_prev <= TQ, d_prev.astype(f32) * (-slopes[h] * dil), NEG)
            bias_ref[h, 1] = jnp.where(d_cur >= 0, d_cur.astype(f32) * (-slopes[h] * dil), NEG)

    pen = jnp.where(j > 0, 0.0, NEG)
    lane = lax.broadcasted_iota(i32, (TQ, LANES), 1)
    lo_half = lane < HEAD_DIM
    lo_half_k = lax.broadcasted_iota(i32, (TQ, LANES), 1) < HEAD_DIM
    if not first:
        st = st_in[0]
    st_new = jnp.zeros((TQ, LANES), f32)
    tiles = []
    for g in range(H_C // 2):
        ts = slice(g * LANES, (g + 1) * LANES)
        qt = q_ref[0, :, ts]
        kpt, kct = kp_ref[0, :, ts], kc_ref[0, :, ts]
        vpt, vct = vp_ref[0, :, ts], vc_ref[0, :, ts]
        zero = jnp.zeros_like(qt)
        one = jnp.ones_like(vpt)
        pv, ms = [], []
        for e in range(2):
            h = 2 * g + e
            own = lo_half if e == 0 else jnp.logical_not(lo_half)
            own_k = lo_half_k if e == 0 else jnp.logical_not(lo_half_k)
            qm = jnp.where(own, qt, zero)
            lg_p = lax.dot_general(qm, kpt, _NT, preferred_element_type=f32) + bias_ref[h, 0] + pen
            lg_c = lax.dot_general(qm, kct, _NT, preferred_element_type=f32) + bias_ref[h, 1]
            m_new = jnp.max(jnp.maximum(lg_p, lg_c), axis=-1, keepdims=True)
            if not first:
                m_old = st[:, h:h + 1]
                m_new = jnp.maximum(m_new, m_old)
            p_p = jnp.exp(lg_p - m_new).astype(bf16)
            p_c = jnp.exp(lg_c - m_new).astype(bf16)
            r = (jnp.dot(p_p, jnp.where(own_k, vpt, one), preferred_element_type=f32)
                 + jnp.dot(p_c, jnp.where(own_k, vct, one), preferred_element_type=f32))
            pv.append(r)
            ms.append(m_new)
        acc_t = jnp.where(lo_half, pv[0], pv[1])
        l_t = pltpu.roll(jnp.where(lo_half, pv[1], pv[0]), HEAD_DIM, 1)
        if not first:
            alpha = jnp.exp(jnp.where(lo_half, st[:, 2 * g:2 * g + 1] - ms[0], st[:, 2 * g + 1:2 * g + 2] - ms[1]))
            l_old = jnp.where(lo_half, st[:, H_C + 2 * g:H_C + 2 * g + 1], st[:, H_C + 2 * g + 1:H_C + 2 * g + 2])
            acc_t = acc_t + alpha * acc_in[0, :, ts]
            l_t = l_t + alpha * l_old
        if last:
            tiles.append(acc_t / l_t)
        else:
            tiles.append(acc_t)
            for e in range(2):
                h = 2 * g + e
                st_new = jnp.where(lane == h, ms[e], st_new)
                st_new = jnp.where(lane == H_C + h, l_t[:, e * HEAD_DIM:e * HEAD_DIM + 1], st_new)
    if last:
        o_ref[0] = jnp.concatenate(tiles, axis=-1)
    else:
        acc_out[0] = jnp.concatenate(tiles, axis=-1)
        st_out[0] = st_new


def _dilated_branch(qkv, acc, st, *, dil, slopes, first, last):
    B, S, W3 = qkv.shape
    W = W3 // 3
    Lc = S // dil
    nj = Lc // TQ
    qkv_v = qkv.reshape(B, Lc, dil * W3)
    blk = lambda comp, prev: pl.BlockSpec(
        (1, TQ, W), (lambda b, r, j: (b, jnp.maximum(j - 1, 0), r * 3 + comp)) if prev
        else (lambda b, r, j: (b, j, r * 3 + comp)))
    acc_spec = pl.BlockSpec((1, TQ, W), lambda b, r, j: (b, j, r))
    st_spec = pl.BlockSpec((1, TQ, LANES), lambda b, r, j: (b, j, r))
    in_specs = [blk(0, False), blk(1, True), blk(1, False), blk(2, True), blk(2, False)]
    args = [qkv_v] * 5
    if not first:
        in_specs += [acc_spec, st_spec]
        args += [acc.reshape(B, Lc, dil * W), st.reshape(B, Lc, dil * LANES)]
    if last:
        out_shape = jax.ShapeDtypeStruct((B, Lc, dil * W), f32)
        out_specs = acc_spec
    else:
        out_shape = [jax.ShapeDtypeStruct((B, Lc, dil * W), f32), jax.ShapeDtypeStruct((B, Lc, dil * LANES), f32)]
        out_specs = [acc_spec, st_spec]
    res = pl.pallas_call(
        functools.partial(_dil_kernel, dil=dil, slopes=slopes, first=first, last=last),
        out_shape=out_shape,
        grid=(B, dil, nj),
        in_specs=in_specs,
        out_specs=out_specs,
        scratch_shapes=[pltpu.VMEM((H_C, 2, TQ, TQ), f32)],
        compiler_params=_params(("arbitrary", "arbitrary", "arbitrary"), 32),
        name=f"dilated_d{dil}",
    )(*args)
    if last:
        return res.reshape(B, S, W)
    return res[0].reshape(B, S, W), res[1].reshape(B, S, LANES)


def _dilated(qkv, slopes):
    order = sorted(DILATIONS, key=lambda wd: -wd[1])
    acc = st = None
    for n, (window, dil) in enumerate(order):
        assert window // dil == TQ
        first, last = n == 0, n == len(order) - 1
        res = _dilated_branch(qkv, acc, st, dil=dil, slopes=slopes, first=first, last=last)
        if last:
            return res
        acc, st = res


def _out_kernel(oa_ref, ob_ref, oc_ref, x_ref, gt_ref, go_ref, w_ref, o_ref):
    go = go_ref[...]
    y = None
    o = 0
    for ref in (oa_ref, ob_ref, oc_ref):
        v = ref[0]
        w = v.shape[1]
        ms = jnp.mean(v * v, axis=-1, keepdims=True)
        yn = (v * lax.rsqrt(ms + EPS) * go[:, o:o + w]).astype(bf16)
        part = jnp.dot(yn, w_ref[o:o + w, :], preferred_element_type=f32)
        y = part if y is None else y + part
        o += w
    o_ref[0] = x_ref[0] + gt_ref[0] * y


def _out_proj(oa, ob, oc, x, gt, go, w, *, tm):
    B, S, D = x.shape
    blk = lambda a: pl.BlockSpec((1, tm, a.shape[2]), lambda b, j: (b, j, 0))
    return pl.pallas_call(
        _out_kernel,
        out_shape=jax.ShapeDtypeStruct((B, S, D), f32),
        grid=(B, S // tm),
        in_specs=[blk(oa), blk(ob), blk(oc), blk(x), pl.BlockSpec((1, 1, D), lambda b, j: (b, 0, 0)),
                  _resident(go.shape), _resident(w.shape)],
        out_specs=blk(x),
        compiler_params=_params(("parallel", "parallel"), 40),
        name="mixer_out_proj",
    )(oa, ob, oc, x, gt, go, w)


def _regroup_w_in(w_in):
    WA, WB, WC = H_A * HEAD_DIM, H_B * HEAD_DIM, H_C * HEAD_DIM
    WI = IDX_HEADS * IDX_DIM
    sizes = [WA] * 3 + [WI, IDX_DIM, IDX_HEADS] + [WB] * 3 + [WC] * 3
    offs = np.concatenate([[0], np.cumsum(sizes)])
    seg = [w_in[:, offs[n]:offs[n + 1]] for n in range(len(sizes))]
    qa, ka, va, qi, ki, wi, qb, kb, vb, qc, kc, vc = seg
    pad = jnp.zeros((w_in.shape[0], LANES - IDX_DIM - IDX_HEADS), w_in.dtype)
    return jnp.concatenate([qa, ka, va, qb, kb, vb, qc, kc, vc, qi, ki, wi, pad], axis=1).astype(bf16)


def kernel(x, c, w_ada, b_ada, norm_ffn1, w_gate1, w_up1, w_down1, norm_mix, w_in, qn_a, kn_a, qn_c, kn_c,
           g_out, w_out, norm_ffn2, w_gate2, w_up2, w_down2):
    B, S, D = x.shape
    L = w_ada.shape[0]
    topk = min(TOPK_MAX, S // 4)
    slopes_a, slopes_c = _alibi_slopes()
    tm = min(512, S)
    mod = _modulation(c, w_ada, b_ada).reshape(L, B, N_MOD, 1, D)
    tile = lambda g, n: jnp.tile(g, n).reshape(1, -1)
    for l in range(L):
        sh1, sc1, g1, sh2, sc2, g2, sh3, sc3, g3 = [mod[l, :, n] for n in range(N_MOD)]
        x = _ffn(x, sh1, sc1, g1, norm_ffn1[l].reshape(1, D), w_gate1[l].astype(bf16), w_up1[l].astype(bf16),
                 w_down1[l].astype(bf16), tm=tm, n_chunks=2)
        qkv_a, qkv_b, qkv_c, qi, kib, kw = _proj(
            x, sh2, sc2, norm_mix[l].reshape(1, D), _regroup_w_in(w_in[l]),
            tile(qn_a[l], H_A), tile(kn_a[l], H_A), tile(qn_c[l], H_C), tile(kn_c[l], H_C), tm=tm)
        o_a = _dsa(qkv_a, qi, kib, kw, topk=topk, slopes=slopes_a, ck=min(512, S))
        o_b = _stick_breaking(qkv_b, kb=256)
        o_c = _dilated(qkv_c, slopes_c)
        x = _out_proj(o_a, o_b, o_c, x, g2, g_out[l].reshape(1, -1), w_out[l].astype(bf16), tm=tm)
        x = _ffn(x, sh3, sc3, g3, norm_ffn2[l].reshape(1, D), w_gate2[l].astype(bf16), w_up2[l].astype(bf16),
                 w_down2[l].astype(bf16), tm=tm, n_chunks=2)
    return x
```

```python
import functools

import numpy as np
import jax
import jax.numpy as jnp
from jax import lax
from jax.experimental import pallas as pl
from jax.experimental.pallas import tpu as pltpu

HEAD_DIM = 64
H_A = 4
H_B = 4
H_C = 8
IDX_HEADS = 8
IDX_DIM = 64
TOPK_MAX = 256
DILATIONS = ((128, 1), (512, 4), (2048, 16))
N_MOD = 9
EPS = 1e-6

TQ = 128
LANES = 128
NEG = -1e30
INT_MIN = -2 ** 31
NEG_INF_KEY = -2139095041
EXP_DEAD = -104.0
V7X_VMEM_BYTES = 64 * 1024 * 1024

f32 = jnp.float32
bf16 = jnp.bfloat16
i32 = jnp.int32

_NT = (((1,), (1,)), ((), ()))


def _params(sem, vmem_mb):
    return pltpu.CompilerParams(dimension_semantics=sem,
                                vmem_limit_bytes=min(vmem_mb * 1024 * 1024, V7X_VMEM_BYTES - (4 << 20)))


def _alibi_slopes():
    n = H_A + H_C
    s = np.exp2(-8.0 * np.arange(1, n + 1, dtype=np.float32) / np.float32(n)).astype(np.float32)
    idx = np.arange(n)
    return [float(v) for v in s[idx % 3 == 2]], [float(v) for v in s[idx % 3 != 2]]


def _resident(shape):
    nd = len(shape)
    return pl.BlockSpec(shape, lambda *_: (0,) * nd, pipeline_mode=pl.Buffered(1))


def _mod_kernel(c_ref, w_ref, b_ref, o_ref):
    c = c_ref[...]
    cond = c * jax.nn.sigmoid(c)
    o_ref[0] = jnp.dot(cond.astype(bf16), w_ref[0].astype(bf16), preferred_element_type=f32) + b_ref[0]


def _modulation(c, w_ada, b_ada):
    L, D, ND = w_ada.shape
    B = c.shape[0]
    tn = D
    return pl.pallas_call(
        _mod_kernel,
        out_shape=jax.ShapeDtypeStruct((L, B, ND), f32),
        grid=(L, ND // tn),
        in_specs=[pl.BlockSpec((B, D), lambda l, n: (0, 0)),
                  pl.BlockSpec((1, D, tn), lambda l, n: (l, 0, n)),
                  pl.BlockSpec((1, 1, tn), lambda l, n: (l, 0, n))],
        out_specs=pl.BlockSpec((1, B, tn), lambda l, n: (l, 0, n)),
        compiler_params=_params(("arbitrary", "arbitrary"), 32),
        name="adaln_mod",
    )(c, w_ada, b_ada.reshape(L, 1, ND))


def _norm_mod(x, nw, sc, sh):
    ms = jnp.mean(x * x, axis=-1, keepdims=True)
    y = x * lax.rsqrt(ms + EPS) * nw
    return y * (1.0 + sc) + sh


def _ffn_kernel(x_ref, sh_ref, sc_ref, gt_ref, nw_ref, wg_ref, wu_ref, wd_ref, o_ref, *, n_chunks):
    x = x_ref[0]
    h = _norm_mod(x, nw_ref[...], sc_ref[0], sh_ref[0]).astype(bf16)
    F = wg_ref.shape[1]
    fc = F // n_chunks
    y = None
    for c in range(n_chunks):
        g = jnp.dot(h, wg_ref[:, c * fc:(c + 1) * fc], preferred_element_type=f32)
        u = jnp.dot(h, wu_ref[:, c * fc:(c + 1) * fc], preferred_element_type=f32)
        a = (g * jax.nn.sigmoid(g) * u).astype(bf16)
        part = jnp.dot(a, wd_ref[c * fc:(c + 1) * fc, :], preferred_element_type=f32)
        y = part if y is None else y + part
    o_ref[0] = x + (0.5 * gt_ref[0]) * y


def _ffn(x, sh, sc, gt, nw, wg, wu, wd, *, tm, n_chunks):
    B, S, D = x.shape
    F = wg.shape[1]
    vec = pl.BlockSpec((1, 1, D), lambda b, j: (b, 0, 0))
    return pl.pallas_call(
        functools.partial(_ffn_kernel, n_chunks=n_chunks),
        out_shape=jax.ShapeDtypeStruct((B, S, D), f32),
        grid=(B, S // tm),
        in_specs=[pl.BlockSpec((1, tm, D), lambda b, j: (b, j, 0)), vec, vec, vec,
                  _resident((1, D)), _resident((D, F)), _resident((D, F)), _resident((F, D))],
        out_specs=pl.BlockSpec((1, tm, D), lambda b, j: (b, j, 0)),
        compiler_params=_params(("parallel", "parallel"), 56),
        name="ffn_swiglu",
    )(x, sh, sc, gt, nw, wg, wu, wd)


def _head_rms(x, gain_row, scale):
    W = x.shape[1]
    outs = []
    r = lax.broadcasted_iota(i32, (256, 256), 0) >> 6
    c = lax.broadcasted_iota(i32, (256, 256), 1) >> 6
    bd = jnp.where(r == c, 1.0, 0.0).astype(bf16)
    for s in range(W // 256):
        xs = x[:, s * 256:(s + 1) * 256]
        sq = xs * xs
        hi = sq.astype(bf16)
        lo = (sq - hi.astype(f32)).astype(bf16)
        ss = jnp.dot(hi, bd, preferred_element_type=f32) + jnp.dot(lo, bd, preferred_element_type=f32)
        y = xs * lax.rsqrt(ss * (1.0 / HEAD_DIM) + EPS) * gain_row[:, s * 256:(s + 1) * 256]
        outs.append(y * scale if scale != 1.0 else y)
    return outs[0] if len(outs) == 1 else jnp.concatenate(outs, axis=-1)


def _proj_kernel(x_ref, sh_ref, sc_ref, nw_ref, w_ref, qna_ref, kna_ref, qnc_ref, knc_ref,
                 a_ref, b_ref, c_ref, qi_ref, kib_ref, kw_ref):
    x = x_ref[0]
    h = _norm_mod(x, nw_ref[...], sc_ref[0], sh_ref[0]).astype(bf16)
    WA, WB, WC = H_A * HEAD_DIM, H_B * HEAD_DIM, H_C * HEAD_DIM
    qscale = HEAD_DIM ** -0.5
    o = 0

    def mm(width):
        nonlocal o
        r = jnp.dot(h, w_ref[:, o:o + width], preferred_element_type=f32)
        o += width
        return r

    pa = mm(3 * WA)
    a_ref[0] = jnp.concatenate([_head_rms(pa[:, :WA], qna_ref[...], qscale),
                                _head_rms(pa[:, WA:2 * WA], kna_ref[...], 1.0),
                                pa[:, 2 * WA:]], axis=-1).astype(bf16)
    pb = mm(3 * WB)
    b_ref[0] = jnp.concatenate([pb[:, :WB] * qscale, pb[:, WB:]], axis=-1).astype(bf16)
    pc = mm(3 * WC)
    c_ref[0] = jnp.concatenate([_head_rms(pc[:, :WC], qnc_ref[...], qscale),
                                _head_rms(pc[:, WC:2 * WC], knc_ref[...], 1.0),
                                pc[:, 2 * WC:]], axis=-1).astype(bf16)
    qi_ref[0] = mm(IDX_HEADS * IDX_DIM).astype(bf16)
    kw = mm(LANES)
    kib_ref[0] = kw.astype(bf16)
    kw_ref[0] = kw


def _proj(x, sh, sc, nw, w2, qna, kna, qnc, knc, *, tm):
    B, S, D = x.shape
    WA, WB, WC = H_A * HEAD_DIM, H_B * HEAD_DIM, H_C * HEAD_DIM
    WI = IDX_HEADS * IDX_DIM
    vec = pl.BlockSpec((1, 1, D), lambda b, j: (b, 0, 0))
    widths = (3 * WA, 3 * WB, 3 * WC, WI, LANES, LANES)
    dtypes = (bf16, bf16, bf16, bf16, bf16, f32)
    return pl.pallas_call(
        _proj_kernel,
        out_shape=[jax.ShapeDtypeStruct((B, S, w), dt) for w, dt in zip(widths, dtypes)],
        grid=(B, S // tm),
        in_specs=[pl.BlockSpec((1, tm, D), lambda b, j: (b, j, 0)), vec, vec,
                  _resident((1, D)), _resident(w2.shape),
                  _resident((1, WA)), _resident((1, WA)), _resident((1, WC)), _resident((1, WC))],
        out_specs=[pl.BlockSpec((1, tm, w), lambda b, j: (b, j, 0)) for w in widths],
        compiler_params=_params(("parallel", "parallel"), 48),
        name="mixer_in_proj",
    )(x, sh, sc, nw, w2, qna, kna, qnc, knc)


def _dsa_kernel(q_ref, k_ref, v_ref, qi_ref, ki_ref, w_ref, o_ref,
                sc_ref, bias_ref, logit_ref, mx_ref, sum_ref, acc_ref, *, topk, slopes, ck):
    assert TQ == LANES
    S = k_ref.shape[1]
    i = pl.program_id(1)
    t0 = i * TQ
    nck = lax.div(t0 + TQ + ck - 1, ck)
    lt = ck // LANES
    tpos = t0 + lax.broadcasted_iota(i32, (TQ, 1), 0)
    lane_pos = lax.broadcasted_iota(i32, (1, ck), 1)
    idx_scale = (IDX_DIM ** -0.5) * (IDX_HEADS ** -0.5)

    def fold(m):
        r = m[:, 0:LANES]
        for t in range(1, lt):
            r = r + m[:, t * LANES:(t + 1) * LANES]
        return r

    def fold_max(x):
        r = x[:, 0:LANES]
        for t in range(1, lt):
            r = jnp.maximum(r, x[:, t * LANES:(t + 1) * LANES])
        return r

    def fold_min(x):
        r = x[:, 0:LANES]
        for t in range(1, lt):
            r = jnp.minimum(r, x[:, t * LANES:(t + 1) * LANES])
        return r

    def to_key(x):
        bits = lax.bitcast_convert_type(x, i32)
        return bits ^ ((bits >> 31) & 0x7FFFFFFF)

    def from_key(k):
        return lax.bitcast_convert_type(k ^ ((k >> 31) & 0x7FFFFFFF), f32)

    def count(pred_fn):
        lane_t = lax.broadcasted_iota(i32, (1, LANES), 1)

        def body(c, acc):
            sc = sc_ref[c]
            for t in range(lt):
                m = pred_fn(sc[:, t * LANES:(t + 1) * LANES], c * ck + t * LANES + lane_t)
                acc = acc + jnp.where(m, 1, 0).astype(i32)
            return acc
        acc = lax.fori_loop(0, nck, body, jnp.zeros((TQ, LANES), i32))
        return jnp.sum(acc.astype(f32), axis=-1, keepdims=True)

    def tile(col):
        return jnp.broadcast_to(col, (TQ, LANES))

    qi = qi_ref[0]
    qstack = jnp.concatenate([qi[:, h * IDX_DIM:(h + 1) * IDX_DIM] for h in range(IDX_HEADS)], axis=0)
    wts = w_ref[0]
    wcols = [wts[:, IDX_DIM + h:IDX_DIM + h + 1] for h in range(IDX_HEADS)]

    def score_body(c, carry):
        c0 = pl.multiple_of(c * ck, ck)
        kic = ki_ref[0, pl.ds(c0, ck), :][:, :IDX_DIM]
        rel = lax.dot_general(qstack, kic, _NT, preferred_element_type=f32)
        score = wcols[0] * jnp.maximum(rel[0:TQ], 0.0)
        for h in range(1, IDX_HEADS):
            score = score + wcols[h] * jnp.maximum(rel[h * TQ:(h + 1) * TQ], 0.0)
        score = score * idx_scale
        score = jnp.where(score == 0.0, 0.0, score)
        sc_ref[c] = jnp.where(c0 + lane_pos <= tpos, score, -jnp.inf)
        return carry

    lax.fori_loop(0, nck, score_body, 0)

    kf = float(topk)

    def bit_body(b, p):
        cand_u = p | lax.shift_left(jnp.int32(1), 31 - b)
        cand = tile(from_key(cand_u ^ INT_MIN))
        cnt = count(lambda sc, kp: sc >= cand)
        return jnp.where(cnt >= kf, cand_u, p)

    tau_key = lax.fori_loop(0, 32, bit_body, jnp.zeros((TQ, 1), i32)) ^ INT_MIN
    tau = from_key(tau_key)
    tau_eff = from_key(jnp.maximum(tau_key, NEG_INF_KEY + 1))

    tau_t = tile(tau)
    cnt_gt = count(lambda sc, kp: sc > tau_t)
    cnt_ge = count(lambda sc, kp: sc >= tau_t)
    need = (cnt_ge > kf) & (tau_key > NEG_INF_KEY)
    any_need = jnp.max(jnp.where(need, 1.0, 0.0)) > 0.0

    @pl.when(jnp.logical_not(any_need))
    def _():
        def bias_body(c, carry):
            bias_ref[c] = jnp.where(sc_ref[c] >= tau_eff, 0.0, NEG)
            return carry

        lax.fori_loop(0, nck, bias_body, 0)

    @pl.when(any_need)
    def _():
        quota = kf - cnt_gt
        nbits = max(1, (S - 1).bit_length())

        def jbit(b, lo):
            cand = tile(lo | lax.shift_left(jnp.int32(1), nbits - 1 - b))
            c = count(lambda sc, kp: (sc == tau_t) & (kp < cand))
            return jnp.where(c < quota, lo | lax.shift_left(jnp.int32(1), nbits - 1 - b), lo)

        jstar = lax.fori_loop(0, nbits, jbit, jnp.zeros((TQ, 1), i32))

        def bias_body(c, carry):
            sc = sc_ref[c]
            kp = c * ck + lane_pos
            tied = (sc > tau) | ((sc == tau) & (kp <= jstar))
            plain = jnp.logical_not(need) & (sc >= tau_eff)
            sel = ((need & tied) | plain) & (kp <= tpos)
            bias_ref[c] = jnp.where(sel, 0.0, NEG)
            return carry

        lax.fori_loop(0, nck, bias_body, 0)

    lane = lax.broadcasted_iota(i32, (1, LANES), 1)
    own = [(lane < HEAD_DIM) if h % 2 == 0 else (lane >= HEAD_DIM) for h in range(H_A)]
    qm = []
    for h in range(H_A):
        qt = q_ref[0, :, (h // 2) * LANES:(h // 2 + 1) * LANES]
        qm.append(jnp.where(own[h], qt, jnp.zeros_like(qt)))

    mx_ref[...] = jnp.full(mx_ref.shape, NEG, f32)

    def logit_body(c, carry):
        c0 = pl.multiple_of(c * ck, ck)
        bias = bias_ref[c]
        pos = (c0 + lane_pos).astype(f32)
        for h in range(H_A):
            kt = k_ref[0, pl.ds(c0, ck), (h // 2) * LANES:(h // 2 + 1) * LANES]
            lg = lax.dot_general(qm[h], kt, _NT, preferred_element_type=f32) + (slopes[h] * pos + bias)
            logit_ref[h, c] = lg
            mx_ref[h] = jnp.maximum(mx_ref[h], fold_max(lg))
        return carry

    lax.fori_loop(0, nck, logit_body, 0)
    row_max = [jnp.max(mx_ref[h], axis=-1, keepdims=True) for h in range(H_A)]
    sum_ref[...] = jnp.zeros(sum_ref.shape, f32)
    acc_ref[...] = jnp.zeros(acc_ref.shape, f32)

    def pv_body(c, carry):
        c0 = pl.multiple_of(c * ck, ck)
        for h in range(H_A):
            g = h // 2
            vt = v_ref[0, pl.ds(c0, ck), g * LANES:(g + 1) * LANES]
            p = jnp.exp(logit_ref[h, c] - row_max[h])
            sum_ref[h] = sum_ref[h] + fold(p)
            acc_ref[g] = acc_ref[g] + jnp.dot(p.astype(bf16), jnp.where(own[h], vt, jnp.zeros_like(vt)),
                                              preferred_element_type=f32)
        return carry

    lax.fori_loop(0, nck, pv_body, 0)
    outs = []
    for g in range(H_A // 2):
        l_e = jnp.sum(sum_ref[2 * g], axis=-1, keepdims=True)
        l_o = jnp.sum(sum_ref[2 * g + 1], axis=-1, keepdims=True)
        outs.append(acc_ref[g] / jnp.where(own[2 * g], l_e, l_o))
    o_ref[0] = jnp.concatenate(outs, axis=-1)


def _dsa(qkv, qi, kib, kw, *, topk, slopes, ck):
    B, S, _ = qkv.shape
    W = H_A * HEAD_DIM
    WI = IDX_HEADS * IDX_DIM
    return pl.pallas_call(
        functools.partial(_dsa_kernel, topk=topk, slopes=slopes, ck=ck),
        out_shape=jax.ShapeDtypeStruct((B, S, W), f32),
        grid=(B, S // TQ),
        in_specs=[pl.BlockSpec((1, TQ, W), lambda b, i: (b, i, 0)),
                  pl.BlockSpec((1, S, W), lambda b, i: (b, 0, 1)),
                  pl.BlockSpec((1, S, W), lambda b, i: (b, 0, 2)),
                  pl.BlockSpec((1, TQ, WI), lambda b, i: (b, i, 0)),
                  pl.BlockSpec((1, S, LANES), lambda b, i: (b, 0, 0)),
                  pl.BlockSpec((1, TQ, LANES), lambda b, i: (b, i, 0))],
        out_specs=pl.BlockSpec((1, TQ, W), lambda b, i: (b, i, 0)),
        scratch_shapes=[pltpu.VMEM((S // ck, TQ, ck), f32),
                        pltpu.VMEM((S // ck, TQ, ck), f32),
                        pltpu.VMEM((H_A, S // ck, TQ, ck), f32),
                        pltpu.VMEM((H_A, TQ, LANES), f32),
                        pltpu.VMEM((H_A, TQ, LANES), f32),
                        pltpu.VMEM((H_A // 2, TQ, LANES), f32)],
        compiler_params=_params(("parallel", "arbitrary"), 48),
        name="dsa_attention",
    )(qkv, qkv, qkv, qi, kib, kw)


def _sb_kernel(q_ref, k_ref, v_ref, o_ref, *, kb):
    i = pl.program_id(1)
    t0 = i * TQ
    nkb = lax.div(t0 + TQ + kb - 1, kb)
    tpos = t0 + lax.broadcasted_iota(i32, (TQ, 1), 0)
    lane_pos = lax.broadcasted_iota(i32, (1, kb), 1)
    tri = jnp.where(lax.broadcasted_iota(i32, (kb, kb), 0) > lax.broadcasted_iota(i32, (kb, kb), 1),
                    1.0, 0.0).astype(bf16)
    q = q_ref[0]
    qs = [q[:, h * HEAD_DIM:(h + 1) * HEAD_DIM] for h in range(H_B)]

    def cond(carry):
        jj, alive = carry[0], carry[1]
        return (jj < nkb) & (alive > EXP_DEAD)

    def body(carry):
        jj, _, runs, accs = carry
        s0 = pl.multiple_of((nkb - 1 - jj) * kb, kb)
        kt = k_ref[0, pl.ds(s0, kb), :]
        vt = v_ref[0, pl.ds(s0, kb), :]
        strict = (s0 + lane_pos) < tpos
        zs, sps, lms, parts = [], [], [], ([], [], [])
        for h in range(H_B):
            hs = slice(h * HEAD_DIM, (h + 1) * HEAD_DIM)
            z = lax.dot_general(qs[h], kt[:, hs], _NT, preferred_element_type=f32)
            sp = jnp.maximum(z, 0.0) + jnp.log1p(jnp.exp(-jnp.abs(z)))
            lm = jnp.where(strict, -sp, 0.0)
            l1 = lm.astype(bf16)
            r1 = lm - l1.astype(f32)
            l2 = r1.astype(bf16)
            l3 = (r1 - l2.astype(f32)).astype(bf16)
            zs.append(z); sps.append(sp); lms.append(lm)
            parts[0].append(l1); parts[1].append(l2); parts[2].append(l3)
        later = sum(jnp.dot(jnp.concatenate(p, axis=0), tri, preferred_element_type=f32) for p in parts)
        new_runs, new_accs = [], []
        alive = None
        for h in range(H_B):
            hs = slice(h * HEAD_DIM, (h + 1) * HEAD_DIM)
            a = jnp.where(strict, jnp.exp((zs[h] - sps[h]) + (runs[h] + later[h * TQ:(h + 1) * TQ])), 0.0)
            new_accs.append(accs[h] + jnp.dot(a.astype(bf16), vt[:, hs], preferred_element_type=f32))
            run = runs[h] + jnp.sum(lms[h], axis=-1, keepdims=True)
            new_runs.append(run)
            alive = run if alive is None else jnp.maximum(alive, run)
        return jj + 1, jnp.max(alive), tuple(new_runs), tuple(new_accs)

    init = (jnp.int32(0), jnp.float32(0.0),
            tuple(jnp.zeros((TQ, 1), f32) for _ in range(H_B)),
            tuple(jnp.zeros((TQ, HEAD_DIM), f32) for _ in range(H_B)))
    _, _, _, accs = lax.while_loop(cond, body, init)
    o_ref[0] = jnp.concatenate(accs, axis=-1)


def _stick_breaking(qkv, *, kb):
    B, S, _ = qkv.shape
    W = H_B * HEAD_DIM
    return pl.pallas_call(
        functools.partial(_sb_kernel, kb=kb),
        out_shape=jax.ShapeDtypeStruct((B, S, W), f32),
        grid=(B, S // TQ),
        in_specs=[pl.BlockSpec((1, TQ, W), lambda b, i: (b, i, 0)),
                  pl.BlockSpec((1, S, W), lambda b, i: (b, 0, 1)),
                  pl.BlockSpec((1, S, W), lambda b, i: (b, 0, 2))],
        out_specs=pl.BlockSpec((1, TQ, W), lambda b, i: (b, i, 0)),
        compiler_params=_params(("parallel", "arbitrary"), 32),
        name="stick_breaking",
    )(qkv, qkv, qkv)


def _dil_kernel(*refs, dil, slopes, first, last):
    bias_ref, lg_ref, pv_ref = refs[-3:]
    if first:
        q_ref, kp_ref, kc_ref, vp_ref, vc_ref, acc_out, st_out = refs[:-3]
    elif last:
        q_ref, kp_ref, kc_ref, vp_ref, vc_ref, acc_in, st_in, o_ref = refs[:-3]
    else:
        q_ref, kp_ref, kc_ref, vp_ref, vc_ref, acc_in, st_in, acc_out, st_out = refs[:-3]
    j = pl.program_id(2)

    @pl.when((pl.program_id(0) == 0) & (pl.program_id(1) == 0) & (j == 0))
    def _():
        u = lax.broadcasted_iota(i32, (TQ, 1), 0)
        c = lax.broadcasted_iota(i32, (1, TQ), 1)
        d_cur = u - c
        d_prev = d_cur + TQ
        for h in range(H_C):
            bias_ref[h, 0] = jnp.where(d_prev <= TQ, d_prev.astype(f32) * (-slopes[h] * dil), NEG)
            bias_ref[h, 1] = jnp.where(d_cur >= 0, d_cur.astype(f32) * (-slopes[h] * dil), NEG)

    pen = jnp.where(j > 0, 0.0, NEG)
    lane = lax.broadcasted_iota(i32, (TQ, LANES), 1)
    lo_half = lane < HEAD_DIM
    owns = (lo_half, jnp.logical_not(lo_half))

    ms = []
    for h in range(H_C):
        ts = slice((h // 2) * LANES, (h // 2 + 1) * LANES)
        qt = q_ref[0, :, ts]
        qm = jnp.where(owns[h % 2], qt, jnp.zeros_like(qt))
        lg_p = lax.dot_general(qm, kp_ref[0, :, ts], _NT, preferred_element_type=f32) + (bias_ref[h, 0] + pen)
        lg_c = lax.dot_general(qm, kc_ref[0, :, ts], _NT, preferred_element_type=f32) + bias_ref[h, 1]
        lg_ref[h, 0] = lg_p
        lg_ref[h, 1] = lg_c
        m = jnp.max(jnp.maximum(lg_p, lg_c), axis=-1, keepdims=True)
        if not first:
            m = jnp.maximum(m, st_in[0, :, h:h + 1])
        ms.append(m)

    for h in range(H_C):
        ts = slice((h // 2) * LANES, (h // 2 + 1) * LANES)
        vpt, vct = vp_ref[0, :, ts], vc_ref[0, :, ts]
        one = jnp.ones_like(vpt)
        p_p = jnp.exp(lg_ref[h, 0] - ms[h]).astype(bf16)
        p_c = jnp.exp(lg_ref[h, 1] - ms[h]).astype(bf16)
        pv_ref[h] = (jnp.dot(p_p, jnp.where(owns[h % 2], vpt, one), preferred_element_type=f32)
                     + jnp.dot(p_c, jnp.where(owns[h % 2], vct, one), preferred_element_type=f32))

    st_new = jnp.zeros((TQ, LANES), f32)
    for g in range(H_C // 2):
        ts = slice(g * LANES, (g + 1) * LANES)
        pv0, pv1 = pv_ref[2 * g], pv_ref[2 * g + 1]
        acc_t = jnp.where(lo_half, pv0, pv1)
        l_t = pltpu.roll(jnp.where(lo_half, pv1, pv0), HEAD_DIM, 1)
        if not first:
            st = st_in[0]
            alpha = jnp.exp(jnp.where(lo_half, st[:, 2 * g:2 * g + 1] - ms[2 * g],
                                      st[:, 2 * g + 1:2 * g + 2] - ms[2 * g + 1]))
            l_old = jnp.where(lo_half, st[:, H_C + 2 * g:H_C + 2 * g + 1], st[:, H_C + 2 * g + 1:H_C + 2 * g + 2])
            acc_t = acc_t + alpha * acc_in[0, :, ts]
            l_t = l_t + alpha * l_old
        if last:
            o_ref[0, :, ts] = acc_t / l_t
        else:
            acc_out[0, :, ts] = acc_t
            for e in range(2):
                h = 2 * g + e
                st_new = jnp.where(lane == h, ms[h], st_new)
                st_new = jnp.where(lane == H_C + h, l_t[:, e * HEAD_DIM:e * HEAD_DIM + 1], st_new)
    if not last:
        st_out[0] = st_new


def _dilated_branch(qkv, acc, st, *, dil, slopes, first, last):
    B, S, W3 = qkv.shape
    W = W3 // 3
    Lc = S // dil
    nj = Lc // TQ
    qkv_v = qkv.reshape(B, Lc, dil * W3)
    blk = lambda comp, prev: pl.BlockSpec(
        (1, TQ, W), (lambda b, r, j: (b, jnp.maximum(j - 1, 0), r * 3 + comp)) if prev
        else (lambda b, r, j: (b, j, r * 3 + comp)))
    acc_spec = pl.BlockSpec((1, TQ, W), lambda b, r, j: (b, j, r))
    st_spec = pl.BlockSpec((1, TQ, LANES), lambda b, r, j: (b, j, r))
    in_specs = [blk(0, False), blk(1, True), blk(1, False), blk(2, True), blk(2, False)]
    args = [qkv_v] * 5
    if not first:
        in_specs += [acc_spec, st_spec]
        args += [acc.reshape(B, Lc, dil * W), st.reshape(B, Lc, dil * LANES)]
    if last:
        out_shape = jax.ShapeDtypeStruct((B, Lc, dil * W), f32)
        out_specs = acc_spec
    else:
        out_shape = [jax.ShapeDtypeStruct((B, Lc, dil * W), f32), jax.ShapeDtypeStruct((B, Lc, dil * LANES), f32)]
        out_specs = [acc_spec, st_spec]
    res = pl.pallas_call(
        functools.partial(_dil_kernel, dil=dil, slopes=slopes, first=first, last=last),
        out_shape=out_shape,
        grid=(B, dil, nj),
        in_specs=in_specs,
        out_specs=out_specs,
        scratch_shapes=[pltpu.VMEM((H_C, 2, TQ, TQ), f32),
                        pltpu.VMEM((H_C, 2, TQ, TQ), f32),
                        pltpu.VMEM((H_C, TQ, LANES), f32)],
        compiler_params=_params(("arbitrary", "arbitrary", "arbitrary"), 32),
        name=f"dilated_d{dil}",
    )(*args)
    if last:
        return res.reshape(B, S, W)
    return res[0].reshape(B, S, W), res[1].reshape(B, S, LANES)


def _dilated(qkv, slopes):
    order = sorted(DILATIONS, key=lambda wd: -wd[1])
    acc = st = None
    for n, (window, dil) in enumerate(order):
        assert window // dil == TQ
        first, last = n == 0, n == len(order) - 1
        res = _dilated_branch(qkv, acc, st, dil=dil, slopes=slopes, first=first, last=last)
        if last:
            return res
        acc, st = res


def _out_kernel(oa_ref, ob_ref, oc_ref, x_ref, gt_ref, go_ref, w_ref, o_ref):
    go = go_ref[...]
    y = None
    o = 0
    for ref in (oa_ref, ob_ref, oc_ref):
        v = ref[0]
        w = v.shape[1]
        ms = jnp.mean(v * v, axis=-1, keepdims=True)
        yn = (v * lax.rsqrt(ms + EPS) * go[:, o:o + w]).astype(bf16)
        part = jnp.dot(yn, w_ref[o:o + w, :], preferred_element_type=f32)
        y = part if y is None else y + part
        o += w
    o_ref[0] = x_ref[0] + gt_ref[0] * y


def _out_proj(oa, ob, oc, x, gt, go, w, *, tm):
    B, S, D = x.shape
    blk = lambda a: pl.BlockSpec((1, tm, a.shape[2]), lambda b, j: (b, j, 0))
    return pl.pallas_call(
        _out_kernel,
        out_shape=jax.ShapeDtypeStruct((B, S, D), f32),
        grid=(B, S // tm),
        in_specs=[blk(oa), blk(ob), blk(oc), blk(x), pl.BlockSpec((1, 1, D), lambda b, j: (b, 0, 0)),
                  _resident(go.shape), _resident(w.shape)],
        out_specs=blk(x),
        compiler_params=_params(("parallel", "parallel"), 40),
        name="mixer_out_proj",
    )(oa, ob, oc, x, gt, go, w)


def _regroup_w_in(w_in):
    WA, WB, WC = H_A * HEAD_DIM, H_B * HEAD_DIM, H_C * HEAD_DIM
    WI = IDX_HEADS * IDX_DIM
    sizes = [WA] * 3 + [WI, IDX_DIM, IDX_HEADS] + [WB] * 3 + [WC] * 3
    offs = np.concatenate([[0], np.cumsum(sizes)])
    seg = [w_in[:, offs[n]:offs[n + 1]] for n in range(len(sizes))]
    qa, ka, va, qi, ki, wi, qb, kb, vb, qc, kc, vc = seg
    pad = jnp.zeros((w_in.shape[0], LANES - IDX_DIM - IDX_HEADS), w_in.dtype)
    return jnp.concatenate([qa, ka, va, qb, kb, vb, qc, kc, vc, qi, ki, wi, pad], axis=1).astype(bf16)


def kernel(x, c, w_ada, b_ada, norm_ffn1, w_gate1, w_up1, w_down1, norm_mix, w_in, qn_a, kn_a, qn_c, kn_c,
           g_out, w_out, norm_ffn2, w_gate2, w_up2, w_down2):
    B, S, D = x.shape
    L = w_ada.shape[0]
    topk = min(TOPK_MAX, S // 4)
    slopes_a, slopes_c = _alibi_slopes()
    tm = min(512, S)
    mod = _modulation(c, w_ada, b_ada).reshape(L, B, N_MOD, 1, D)
    tile = lambda g, n: jnp.tile(g, n).reshape(1, -1)
    for l in range(L):
        sh1, sc1, g1, sh2, sc2, g2, sh3, sc3, g3 = [mod[l, :, n] for n in range(N_MOD)]
        x = _ffn(x, sh1, sc1, g1, norm_ffn1[l].reshape(1, D), w_gate1[l].astype(bf16), w_up1[l].astype(bf16),
                 w_down1[l].astype(bf16), tm=tm, n_chunks=2)
        qkv_a, qkv_b, qkv_c, qi, kib, kw = _proj(
            x, sh2, sc2, norm_mix[l].reshape(1, D), _regroup_w_in(w_in[l]),
            tile(qn_a[l], H_A), tile(kn_a[l], H_A), tile(qn_c[l], H_C), tile(kn_c[l], H_C), tm=tm)
        o_a = _dsa(qkv_a, qi, kib, kw, topk=topk, slopes=slopes_a, ck=min(512, S))
        o_b = _stick_breaking(qkv_b, kb=256)
        o_c = _dilated(qkv_c, slopes_c)
        x = _out_proj(o_a, o_b, o_c, x, g2, g_out[l].reshape(1, -1), w_out[l].astype(bf16), tm=tm)
        x = _ffn(x, sh3, sc3, g3, norm_ffn2[l].reshape(1, D), w_gate2[l].astype(bf16), w_up2[l].astype(bf16),
                 w_down2[l].astype(bf16), tm=tm, n_chunks=2)
    return x
```

```python
import functools

import numpy as np
import jax
import jax.numpy as jnp
from jax import lax
from jax.experimental import pallas as pl
from jax.experimental.pallas import tpu as pltpu

HEAD_DIM = 64
H_A = 4
H_B = 4
H_C = 8
IDX_HEADS = 8
IDX_DIM = 64
TOPK_MAX = 256
DILATIONS = ((128, 1), (512, 4), (2048, 16))
N_MOD = 9
EPS = 1e-6

TQ = 128
LANES = 128
NEG = -1e30
INT_MIN = -2 ** 31
NEG_INF_KEY = -2139095041
EXP_DEAD = -104.0
V7X_VMEM_BYTES = 64 * 1024 * 1024

f32 = jnp.float32
bf16 = jnp.bfloat16
i32 = jnp.int32

_NT = (((1,), (1,)), ((), ()))


def _params(sem, vmem_mb):
    return pltpu.CompilerParams(dimension_semantics=sem,
                                vmem_limit_bytes=min(vmem_mb * 1024 * 1024, V7X_VMEM_BYTES - (4 << 20)))


def _alibi_slopes():
    n = H_A + H_C
    s = np.exp2(-8.0 * np.arange(1, n + 1, dtype=np.float32) / np.float32(n)).astype(np.float32)
    idx = np.arange(n)
    return [float(v) for v in s[idx % 3 == 2]], [float(v) for v in s[idx % 3 != 2]]


def _resident(shape):
    nd = len(shape)
    return pl.BlockSpec(shape, lambda *_: (0,) * nd, pipeline_mode=pl.Buffered(1))


def _mod_kernel(c_ref, w_ref, b_ref, o_ref):
    c = c_ref[...]
    cond = c * jax.nn.sigmoid(c)
    o_ref[0] = jnp.dot(cond.astype(bf16), w_ref[0].astype(bf16), preferred_element_type=f32) + b_ref[0]


def _modulation(c, w_ada, b_ada):
    L, D, ND = w_ada.shape
    B = c.shape[0]
    tn = D
    return pl.pallas_call(
        _mod_kernel,
        out_shape=jax.ShapeDtypeStruct((L, B, ND), f32),
        grid=(L, ND // tn),
        in_specs=[pl.BlockSpec((B, D), lambda l, n: (0, 0)),
                  pl.BlockSpec((1, D, tn), lambda l, n: (l, 0, n)),
                  pl.BlockSpec((1, 1, tn), lambda l, n: (l, 0, n))],
        out_specs=pl.BlockSpec((1, B, tn), lambda l, n: (l, 0, n)),
        compiler_params=_params(("arbitrary", "arbitrary"), 32),
        name="adaln_mod",
    )(c, w_ada, b_ada.reshape(L, 1, ND))


def _norm_mod(x, nw, sc, sh):
    ms = jnp.mean(x * x, axis=-1, keepdims=True)
    y = x * lax.rsqrt(ms + EPS) * nw
    return y * (1.0 + sc) + sh


def _ffn_kernel(x_ref, sh_ref, sc_ref, gt_ref, nw_ref, wg_ref, wu_ref, wd_ref, o_ref, *, n_chunks):
    x = x_ref[0]
    h = _norm_mod(x, nw_ref[...], sc_ref[0], sh_ref[0]).astype(bf16)
    F = wg_ref.shape[1]
    fc = F // n_chunks
    y = None
    for c in range(n_chunks):
        g = jnp.dot(h, wg_ref[:, c * fc:(c + 1) * fc], preferred_element_type=f32)
        u = jnp.dot(h, wu_ref[:, c * fc:(c + 1) * fc], preferred_element_type=f32)
        a = (g * jax.nn.sigmoid(g) * u).astype(bf16)
        part = jnp.dot(a, wd_ref[c * fc:(c + 1) * fc, :], preferred_element_type=f32)
        y = part if y is None else y + part
    o_ref[0] = x + (0.5 * gt_ref[0]) * y


def _ffn(x, sh, sc, gt, nw, wg, wu, wd, *, tm, n_chunks):
    B, S, D = x.shape
    F = wg.shape[1]
    vec = pl.BlockSpec((1, 1, D), lambda b, j: (b, 0, 0))
    return pl.pallas_call(
        functools.partial(_ffn_kernel, n_chunks=n_chunks),
        out_shape=jax.ShapeDtypeStruct((B, S, D), f32),
        grid=(B, S // tm),
        in_specs=[pl.BlockSpec((1, tm, D), lambda b, j: (b, j, 0)), vec, vec, vec,
                  _resident((1, D)), _resident((D, F)), _resident((D, F)), _resident((F, D))],
        out_specs=pl.BlockSpec((1, tm, D), lambda b, j: (b, j, 0)),
        compiler_params=_params(("parallel", "parallel"), 56),
        name="ffn_swiglu",
    )(x, sh, sc, gt, nw, wg, wu, wd)


def _head_rms(x, gain_row, scale):
    W = x.shape[1]
    outs = []
    r = lax.broadcasted_iota(i32, (256, 256), 0) >> 6
    c = lax.broadcasted_iota(i32, (256, 256), 1) >> 6
    bd = jnp.where(r == c, 1.0, 0.0).astype(bf16)
    for s in range(W // 256):
        xs = x[:, s * 256:(s + 1) * 256]
        sq = xs * xs
        hi = sq.astype(bf16)
        lo = (sq - hi.astype(f32)).astype(bf16)
        ss = jnp.dot(hi, bd, preferred_element_type=f32) + jnp.dot(lo, bd, preferred_element_type=f32)
        y = xs * lax.rsqrt(ss * (1.0 / HEAD_DIM) + EPS) * gain_row[:, s * 256:(s + 1) * 256]
        outs.append(y * scale if scale != 1.0 else y)
    return outs[0] if len(outs) == 1 else jnp.concatenate(outs, axis=-1)


def _proj_kernel(x_ref, sh_ref, sc_ref, nw_ref, w_ref, qna_ref, kna_ref, qnc_ref, knc_ref,
                 a_ref, b_ref, c_ref, qi_ref, kib_ref, kw_ref, *rest, dils):
    class_refs, slab_ref = rest[:-1], rest[-1]
    tm = x_ref.shape[1]
    x = x_ref[0]
    h = _norm_mod(x, nw_ref[...], sc_ref[0], sh_ref[0]).astype(bf16)
    WA, WB, WC = H_A * HEAD_DIM, H_B * HEAD_DIM, H_C * HEAD_DIM
    qscale = HEAD_DIM ** -0.5
    o = 0

    def mm(width):
        nonlocal o
        r = jnp.dot(h, w_ref[:, o:o + width], preferred_element_type=f32)
        o += width
        return r

    pa = mm(3 * WA)
    a_ref[0] = jnp.concatenate([_head_rms(pa[:, :WA], qna_ref[...], qscale),
                                _head_rms(pa[:, WA:2 * WA], kna_ref[...], 1.0),
                                pa[:, 2 * WA:]], axis=-1).astype(bf16)
    pb = mm(3 * WB)
    b_ref[0] = jnp.concatenate([pb[:, :WB] * qscale, pb[:, WB:]], axis=-1).astype(bf16)
    pc = mm(3 * WC)
    qkv_c = jnp.concatenate([_head_rms(pc[:, :WC], qnc_ref[...], qscale),
                             _head_rms(pc[:, WC:2 * WC], knc_ref[...], 1.0),
                             pc[:, 2 * WC:]], axis=-1)
    c_ref[0] = qkv_c.astype(bf16)
    n_slabs = 3 * WC // LANES
    for g in range(n_slabs):
        slab_ref[g] = qkv_c[:, g * LANES:(g + 1) * LANES]
    for ref, dil in zip(class_refs, dils):
        for r in range(dil):
            for g in range(n_slabs):
                ref[0, r, :, g * LANES:(g + 1) * LANES] = slab_ref[g, pl.ds(r, tm // dil, stride=dil), :].astype(bf16)
    qi_ref[0] = mm(IDX_HEADS * IDX_DIM).astype(bf16)
    kw = mm(LANES)
    kib_ref[0] = kw.astype(bf16)
    kw_ref[0] = kw


def _proj(x, sh, sc, nw, w2, qna, kna, qnc, knc, *, tm, dils):
    B, S, D = x.shape
    WA, WB, WC = H_A * HEAD_DIM, H_B * HEAD_DIM, H_C * HEAD_DIM
    WI = IDX_HEADS * IDX_DIM
    vec = pl.BlockSpec((1, 1, D), lambda b, j: (b, 0, 0))
    widths = (3 * WA, 3 * WB, 3 * WC, WI, LANES, LANES)
    dtypes = (bf16, bf16, bf16, bf16, bf16, f32)
    out_shape = [jax.ShapeDtypeStruct((B, S, w), dt) for w, dt in zip(widths, dtypes)]
    out_specs = [pl.BlockSpec((1, tm, w), lambda b, j: (b, j, 0)) for w in widths]
    for dil in dils:
        out_shape.append(jax.ShapeDtypeStruct((B, dil, S // dil, 3 * WC), bf16))
        out_specs.append(pl.BlockSpec((1, dil, tm // dil, 3 * WC), lambda b, j: (b, 0, j, 0)))
    return pl.pallas_call(
        functools.partial(_proj_kernel, dils=dils),
        out_shape=out_shape,
        grid=(B, S // tm),
        in_specs=[pl.BlockSpec((1, tm, D), lambda b, j: (b, j, 0)), vec, vec,
                  _resident((1, D)), _resident(w2.shape),
                  _resident((1, WA)), _resident((1, WA)), _resident((1, WC)), _resident((1, WC))],
        out_specs=out_specs,
        scratch_shapes=[pltpu.VMEM((3 * WC // LANES, tm, LANES), f32)],
        compiler_params=_params(("parallel", "parallel"), 56),
        name="mixer_in_proj",
    )(x, sh, sc, nw, w2, qna, kna, qnc, knc)


def _dsa_kernel(q_ref, k_ref, v_ref, qi_ref, ki_ref, w_ref, o_ref,
                sc_ref, bias_ref, logit_ref, mx_ref, sum_ref, acc_ref, *, topk, slopes, ck):
    assert TQ == LANES
    S = k_ref.shape[1]
    i = pl.program_id(1)
    t0 = i * TQ
    nck = lax.div(t0 + TQ + ck - 1, ck)
    lt = ck // LANES
    tpos = t0 + lax.broadcasted_iota(i32, (TQ, 1), 0)
    lane_pos = lax.broadcasted_iota(i32, (1, ck), 1)
    idx_scale = (IDX_DIM ** -0.5) * (IDX_HEADS ** -0.5)

    def fold(m):
        r = m[:, 0:LANES]
        for t in range(1, lt):
            r = r + m[:, t * LANES:(t + 1) * LANES]
        return r

    def fold_max(x):
        r = x[:, 0:LANES]
        for t in range(1, lt):
            r = jnp.maximum(r, x[:, t * LANES:(t + 1) * LANES])
        return r

    def fold_min(x):
        r = x[:, 0:LANES]
        for t in range(1, lt):
            r = jnp.minimum(r, x[:, t * LANES:(t + 1) * LANES])
        return r

    def to_key(x):
        bits = lax.bitcast_convert_type(x, i32)
        return bits ^ ((bits >> 31) & 0x7FFFFFFF)

    def from_key(k):
        return lax.bitcast_convert_type(k ^ ((k >> 31) & 0x7FFFFFFF), f32)

    def count(pred_fn):
        lane_t = lax.broadcasted_iota(i32, (1, LANES), 1)

        def body(c, acc):
            sc = sc_ref[c]
            for t in range(lt):
                m = pred_fn(sc[:, t * LANES:(t + 1) * LANES], c * ck + t * LANES + lane_t)
                acc = acc + jnp.where(m, 1, 0).astype(i32)
            return acc
        acc = lax.fori_loop(0, nck, body, jnp.zeros((TQ, LANES), i32))
        return jnp.sum(acc.astype(f32), axis=-1, keepdims=True)

    def tile(col):
        return jnp.broadcast_to(col, (TQ, LANES))

    qi = qi_ref[0]
    qstack = jnp.concatenate([qi[:, h * IDX_DIM:(h + 1) * IDX_DIM] for h in range(IDX_HEADS)], axis=0)
    wts = w_ref[0]
    wcols = [wts[:, IDX_DIM + h:IDX_DIM + h + 1] for h in range(IDX_HEADS)]

    def score_body(c, carry):
        c0 = pl.multiple_of(c * ck, ck)
        kic = ki_ref[0, pl.ds(c0, ck), :][:, :IDX_DIM]
        rel = lax.dot_general(qstack, kic, _NT, preferred_element_type=f32)
        score = wcols[0] * jnp.maximum(rel[0:TQ], 0.0)
        for h in range(1, IDX_HEADS):
            score = score + wcols[h] * jnp.maximum(rel[h * TQ:(h + 1) * TQ], 0.0)
        score = score * idx_scale
        score = jnp.where(score == 0.0, 0.0, score)
        sc_ref[c] = jnp.where(c0 + lane_pos <= tpos, score, -jnp.inf)
        return carry

    lax.fori_loop(0, nck, score_body, 0)

    kf = float(topk)

    def bit_body(b, p):
        cand_u = p | lax.shift_left(jnp.int32(1), 31 - b)
        cand = tile(from_key(cand_u ^ INT_MIN))
        cnt = count(lambda sc, kp: sc >= cand)
        return jnp.where(cnt >= kf, cand_u, p)

    tau_key = lax.fori_loop(0, 32, bit_body, jnp.zeros((TQ, 1), i32)) ^ INT_MIN
    tau = from_key(tau_key)
    tau_eff = from_key(jnp.maximum(tau_key, NEG_INF_KEY + 1))

    tau_t = tile(tau)
    cnt_gt = count(lambda sc, kp: sc > tau_t)
    cnt_ge = count(lambda sc, kp: sc >= tau_t)
    need = (cnt_ge > kf) & (tau_key > NEG_INF_KEY)
    any_need = jnp.max(jnp.where(need, 1.0, 0.0)) > 0.0

    @pl.when(jnp.logical_not(any_need))
    def _():
        def bias_body(c, carry):
            bias_ref[c] = jnp.where(sc_ref[c] >= tau_eff, 0.0, NEG)
            return carry

        lax.fori_loop(0, nck, bias_body, 0)

    @pl.when(any_need)
    def _():
        quota = kf - cnt_gt
        nbits = max(1, (S - 1).bit_length())

        def jbit(b, lo):
            cand = tile(lo | lax.shift_left(jnp.int32(1), nbits - 1 - b))
            c = count(lambda sc, kp: (sc == tau_t) & (kp < cand))
            return jnp.where(c < quota, lo | lax.shift_left(jnp.int32(1), nbits - 1 - b), lo)

        jstar = lax.fori_loop(0, nbits, jbit, jnp.zeros((TQ, 1), i32))

        def bias_body(c, carry):
            sc = sc_ref[c]
            kp = c * ck + lane_pos
            tied = (sc > tau) | ((sc == tau) & (kp <= jstar))
            plain = jnp.logical_not(need) & (sc >= tau_eff)
            sel = ((need & tied) | plain) & (kp <= tpos)
            bias_ref[c] = jnp.where(sel, 0.0, NEG)
            return carry

        lax.fori_loop(0, nck, bias_body, 0)

    lane = lax.broadcasted_iota(i32, (1, LANES), 1)
    own = [(lane < HEAD_DIM) if h % 2 == 0 else (lane >= HEAD_DIM) for h in range(H_A)]
    qm = []
    for h in range(H_A):
        qt = q_ref[0, :, (h // 2) * LANES:(h // 2 + 1) * LANES]
        qm.append(jnp.where(own[h], qt, jnp.zeros_like(qt)))

    mx_ref[...] = jnp.full(mx_ref.shape, NEG, f32)

    def logit_body(c, carry):
        c0 = pl.multiple_of(c * ck, ck)
        bias = bias_ref[c]
        pos = (c0 + lane_pos).astype(f32)
        for h in range(H_A):
            kt = k_ref[0, pl.ds(c0, ck), (h // 2) * LANES:(h // 2 + 1) * LANES]
            lg = lax.dot_general(qm[h], kt, _NT, preferred_element_type=f32) + (slopes[h] * pos + bias)
            logit_ref[h, c] = lg
            mx_ref[h] = jnp.maximum(mx_ref[h], fold_max(lg))
        return carry

    lax.fori_loop(0, nck, logit_body, 0)
    row_max = [jnp.max(mx_ref[h], axis=-1, keepdims=True) for h in range(H_A)]
    sum_ref[...] = jnp.zeros(sum_ref.shape, f32)
    acc_ref[...] = jnp.zeros(acc_ref.shape, f32)

    def pv_body(c, carry):
        c0 = pl.multiple_of(c * ck, ck)
        for h in range(H_A):
            g = h // 2
            vt = v_ref[0, pl.ds(c0, ck), g * LANES:(g + 1) * LANES]
            p = jnp.exp(logit_ref[h, c] - row_max[h])
            sum_ref[h] = sum_ref[h] + fold(p)
            acc_ref[g] = acc_ref[g] + jnp.dot(p.astype(bf16), jnp.where(own[h], vt, jnp.zeros_like(vt)),
                                              preferred_element_type=f32)
        return carry

    lax.fori_loop(0, nck, pv_body, 0)
    outs = []
    for g in range(H_A // 2):
        l_e = jnp.sum(sum_ref[2 * g], axis=-1, keepdims=True)
        l_o = jnp.sum(sum_ref[2 * g + 1], axis=-1, keepdims=True)
        outs.append(acc_ref[g] / jnp.where(own[2 * g], l_e, l_o))
    o_ref[0] = jnp.concatenate(outs, axis=-1)


def _dsa(qkv, qi, kib, kw, *, topk, slopes, ck):
    B, S, _ = qkv.shape
    W = H_A * HEAD_DIM
    WI = IDX_HEADS * IDX_DIM
    return pl.pallas_call(
        functools.partial(_dsa_kernel, topk=topk, slopes=slopes, ck=ck),
        out_shape=jax.ShapeDtypeStruct((B, S, W), f32),
        grid=(B, S // TQ),
        in_specs=[pl.BlockSpec((1, TQ, W), lambda b, i: (b, i, 0)),
                  pl.BlockSpec((1, S, W), lambda b, i: (b, 0, 1)),
                  pl.BlockSpec((1, S, W), lambda b, i: (b, 0, 2)),
                  pl.BlockSpec((1, TQ, WI), lambda b, i: (b, i, 0)),
                  pl.BlockSpec((1, S, LANES), lambda b, i: (b, 0, 0)),
                  pl.BlockSpec((1, TQ, LANES), lambda b, i: (b, i, 0))],
        out_specs=pl.BlockSpec((1, TQ, W), lambda b, i: (b, i, 0)),
        scratch_shapes=[pltpu.VMEM((S // ck, TQ, ck), f32),
                        pltpu.VMEM((S // ck, TQ, ck), f32),
                        pltpu.VMEM((H_A, S // ck, TQ, ck), f32),
                        pltpu.VMEM((H_A, TQ, LANES), f32),
                        pltpu.VMEM((H_A, TQ, LANES), f32),
                        pltpu.VMEM((H_A // 2, TQ, LANES), f32)],
        compiler_params=_params(("parallel", "arbitrary"), 48),
        name="dsa_attention",
    )(qkv, qkv, qkv, qi, kib, kw)


def _sb_kernel(q_ref, k_ref, v_ref, o_ref, *, kb):
    i = pl.program_id(1)
    t0 = i * TQ
    nkb = lax.div(t0 + TQ + kb - 1, kb)
    tpos = t0 + lax.broadcasted_iota(i32, (TQ, 1), 0)
    lane_pos = lax.broadcasted_iota(i32, (1, kb), 1)
    tri = jnp.where(lax.broadcasted_iota(i32, (kb, kb), 0) > lax.broadcasted_iota(i32, (kb, kb), 1),
                    1.0, 0.0).astype(bf16)
    q = q_ref[0]
    qs = [q[:, h * HEAD_DIM:(h + 1) * HEAD_DIM] for h in range(H_B)]

    def cond(carry):
        jj, alive = carry[0], carry[1]
        return (jj < nkb) & (alive > EXP_DEAD)

    def body(carry):
        jj, _, runs, accs = carry
        s0 = pl.multiple_of((nkb - 1 - jj) * kb, kb)
        kt = k_ref[0, pl.ds(s0, kb), :]
        vt = v_ref[0, pl.ds(s0, kb), :]
        strict = (s0 + lane_pos) < tpos
        zs, sps, lms, parts = [], [], [], ([], [], [])
        for h in range(H_B):
            hs = slice(h * HEAD_DIM, (h + 1) * HEAD_DIM)
            z = lax.dot_general(qs[h], kt[:, hs], _NT, preferred_element_type=f32)
            sp = jnp.maximum(z, 0.0) + jnp.log1p(jnp.exp(-jnp.abs(z)))
            lm = jnp.where(strict, -sp, 0.0)
            l1 = lm.astype(bf16)
            r1 = lm - l1.astype(f32)
            l2 = r1.astype(bf16)
            l3 = (r1 - l2.astype(f32)).astype(bf16)
            zs.append(z); sps.append(sp); lms.append(lm)
            parts[0].append(l1); parts[1].append(l2); parts[2].append(l3)
        later = sum(jnp.dot(jnp.concatenate(p, axis=0), tri, preferred_element_type=f32) for p in parts)
        new_runs, new_accs = [], []
        alive = None
        for h in range(H_B):
            hs = slice(h * HEAD_DIM, (h + 1) * HEAD_DIM)
            a = jnp.where(strict, jnp.exp((zs[h] - sps[h]) + (runs[h] + later[h * TQ:(h + 1) * TQ])), 0.0)
            new_accs.append(accs[h] + jnp.dot(a.astype(bf16), vt[:, hs], preferred_element_type=f32))
            run = runs[h] + jnp.sum(lms[h], axis=-1, keepdims=True)
            new_runs.append(run)
            alive = run if alive is None else jnp.maximum(alive, run)
        return jj + 1, jnp.max(alive), tuple(new_runs), tuple(new_accs)

    init = (jnp.int32(0), jnp.float32(0.0),
            tuple(jnp.zeros((TQ, 1), f32) for _ in range(H_B)),
            tuple(jnp.zeros((TQ, HEAD_DIM), f32) for _ in range(H_B)))
    _, _, _, accs = lax.while_loop(cond, body, init)
    o_ref[0] = jnp.concatenate(accs, axis=-1)


def _stick_breaking(qkv, *, kb):
    B, S, _ = qkv.shape
    W = H_B * HEAD_DIM
    return pl.pallas_call(
        functools.partial(_sb_kernel, kb=kb),
        out_shape=jax.ShapeDtypeStruct((B, S, W), f32),
        grid=(B, S // TQ),
        in_specs=[pl.BlockSpec((1, TQ, W), lambda b, i: (b, i, 0)),
                  pl.BlockSpec((1, S, W), lambda b, i: (b, 0, 1)),
                  pl.BlockSpec((1, S, W), lambda b, i: (b, 0, 2))],
        out_specs=pl.BlockSpec((1, TQ, W), lambda b, i: (b, i, 0)),
        compiler_params=_params(("parallel", "arbitrary"), 32),
        name="stick_breaking",
    )(qkv, qkv, qkv)


def _dil_kernel(*refs, dil, slopes, first, last):
    bias_ref, lg_ref, pv_ref = refs[-3:]
    if first:
        q_ref, kp_ref, kc_ref, vp_ref, vc_ref, acc_out, st_out = refs[:-3]
    elif last:
        q_ref, kp_ref, kc_ref, vp_ref, vc_ref, acc_in, st_in, o_ref = refs[:-3]
    else:
        q_ref, kp_ref, kc_ref, vp_ref, vc_ref, acc_in, st_in, acc_out, st_out = refs[:-3]
    j = pl.program_id(1)
    r = pl.program_id(2)
    rows = pl.ds(r, TQ, stride=dil) if dil > 1 else slice(None)

    @pl.when((pl.program_id(0) == 0) & (j == 0) & (r == 0))
    def _():
        u = lax.broadcasted_iota(i32, (TQ, 1), 0)
        c = lax.broadcasted_iota(i32, (1, TQ), 1)
        d_cur = u - c
        d_prev = d_cur + TQ
        for h in range(H_C):
            bias_ref[h, 0] = jnp.where(d_prev <= TQ, d_prev.astype(f32) * (-slopes[h] * dil), NEG)
            bias_ref[h, 1] = jnp.where(d_cur >= 0, d_cur.astype(f32) * (-slopes[h] * dil), NEG)

    pen = jnp.where(j > 0, 0.0, NEG)
    lane = lax.broadcasted_iota(i32, (TQ, LANES), 1)
    lo_half = lane < HEAD_DIM
    owns = (lo_half, jnp.logical_not(lo_half))

    if not first:
        st = st_in[0, rows, :]

    ms = []
    for h in range(H_C):
        ts = slice((h // 2) * LANES, (h // 2 + 1) * LANES)
        qt = q_ref[0, 0, :, ts]
        qm = jnp.where(owns[h % 2], qt, jnp.zeros_like(qt))
        lg_p = lax.dot_general(qm, kp_ref[0, 0, :, ts], _NT, preferred_element_type=f32) + (bias_ref[h, 0] + pen)
        lg_c = lax.dot_general(qm, kc_ref[0, 0, :, ts], _NT, preferred_element_type=f32) + bias_ref[h, 1]
        lg_ref[h, 0] = lg_p
        lg_ref[h, 1] = lg_c
        m = jnp.max(jnp.maximum(lg_p, lg_c), axis=-1, keepdims=True)
        if not first:
            m = jnp.maximum(m, st[:, h:h + 1])
        ms.append(m)

    for h in range(H_C):
        ts = slice((h // 2) * LANES, (h // 2 + 1) * LANES)
        vpt, vct = vp_ref[0, 0, :, ts], vc_ref[0, 0, :, ts]
        one = jnp.ones_like(vpt)
        p_p = jnp.exp(lg_ref[h, 0] - ms[h]).astype(bf16)
        p_c = jnp.exp(lg_ref[h, 1] - ms[h]).astype(bf16)
        pv_ref[h] = (jnp.dot(p_p, jnp.where(owns[h % 2], vpt, one), preferred_element_type=f32)
                     + jnp.dot(p_c, jnp.where(owns[h % 2], vct, one), preferred_element_type=f32))

    st_new = jnp.zeros((TQ, LANES), f32)
    for g in range(H_C // 2):
        ts = slice(g * LANES, (g + 1) * LANES)
        pv0, pv1 = pv_ref[2 * g], pv_ref[2 * g + 1]
        acc_t = jnp.where(lo_half, pv0, pv1)
        l_t = pltpu.roll(jnp.where(lo_half, pv1, pv0), HEAD_DIM, 1)
        if not first:
            alpha = jnp.exp(jnp.where(lo_half, st[:, 2 * g:2 * g + 1] - ms[2 * g],
                                      st[:, 2 * g + 1:2 * g + 2] - ms[2 * g + 1]))
            l_old = jnp.where(lo_half, st[:, H_C + 2 * g:H_C + 2 * g + 1], st[:, H_C + 2 * g + 1:H_C + 2 * g + 2])
            acc_t = acc_t + alpha * acc_in[0, g, rows, :]
            l_t = l_t + alpha * l_old
        if last:
            o_ref[0, :, ts] = acc_t / l_t
        else:
            acc_out[0, g, rows, :] = acc_t
            for e in range(2):
                h = 2 * g + e
                st_new = jnp.where(lane == h, ms[h], st_new)
                st_new = jnp.where(lane == H_C + h, l_t[:, e * HEAD_DIM:e * HEAD_DIM + 1], st_new)
    if not last:
        st_out[0, rows, :] = st_new


def _dilated_branch(qkv_cls, acc, st, *, slopes, first, last):
    B, dil, Lc, W3 = qkv_cls.shape
    W = W3 // 3
    S = dil * Lc
    nj = Lc // TQ
    n_slabs = W // LANES
    assert last == (dil == 1)
    blk = lambda comp, prev: pl.BlockSpec(
        (1, 1, TQ, W), (lambda b, j, r: (b, r, jnp.maximum(j - 1, 0), comp)) if prev
        else (lambda b, j, r: (b, r, j, comp)))
    acc_spec = pl.BlockSpec((1, n_slabs, dil * TQ, LANES), lambda b, j, r: (b, 0, j, 0))
    st_spec = pl.BlockSpec((1, dil * TQ, LANES), lambda b, j, r: (b, j, 0))
    in_specs = [blk(0, False), blk(1, True), blk(1, False), blk(2, True), blk(2, False)]
    args = [qkv_cls] * 5
    if not first:
        in_specs += [acc_spec, st_spec]
        args += [acc, st]
    if last:
        out_shape = jax.ShapeDtypeStruct((B, S, W), f32)
        out_specs = pl.BlockSpec((1, TQ, W), lambda b, j, r: (b, j, 0))
    else:
        out_shape = [jax.ShapeDtypeStruct((B, n_slabs, S, LANES), f32), jax.ShapeDtypeStruct((B, S, LANES), f32)]
        out_specs = [acc_spec, st_spec]
    return pl.pallas_call(
        functools.partial(_dil_kernel, dil=dil, slopes=slopes, first=first, last=last),
        out_shape=out_shape,
        grid=(B, nj, dil),
        in_specs=in_specs,
        out_specs=out_specs,
        scratch_shapes=[pltpu.VMEM((H_C, 2, TQ, TQ), f32),
                        pltpu.VMEM((H_C, 2, TQ, TQ), f32),
                        pltpu.VMEM((H_C, TQ, LANES), f32)],
        compiler_params=_params(("arbitrary", "arbitrary", "arbitrary"), 40),
        name=f"dilated_d{dil}",
    )(*args)


def _dilated(qkv_by_dil, slopes):
    dils = sorted(qkv_by_dil, reverse=True)
    acc = st = None
    for n, dil in enumerate(dils):
        first, last = n == 0, n == len(dils) - 1
        res = _dilated_branch(qkv_by_dil[dil], acc, st, slopes=slopes, first=first, last=last)
        if last:
            return res
        acc, st = res


def _out_kernel(oa_ref, ob_ref, oc_ref, x_ref, gt_ref, go_ref, w_ref, o_ref):
    go = go_ref[...]
    y = None
    o = 0
    for ref in (oa_ref, ob_ref, oc_ref):
        v = ref[0]
        w = v.shape[1]
        ms = jnp.mean(v * v, axis=-1, keepdims=True)
        yn = (v * lax.rsqrt(ms + EPS) * go[:, o:o + w]).astype(bf16)
        part = jnp.dot(yn, w_ref[o:o + w, :], preferred_element_type=f32)
        y = part if y is None else y + part
        o += w
    o_ref[0] = x_ref[0] + gt_ref[0] * y


def _out_proj(oa, ob, oc, x, gt, go, w, *, tm):
    B, S, D = x.shape
    blk = lambda a: pl.BlockSpec((1, tm, a.shape[2]), lambda b, j: (b, j, 0))
    return pl.pallas_call(
        _out_kernel,
        out_shape=jax.ShapeDtypeStruct((B, S, D), f32),
        grid=(B, S // tm),
        in_specs=[blk(oa), blk(ob), blk(oc), blk(x), pl.BlockSpec((1, 1, D), lambda b, j: (b, 0, 0)),
                  _resident(go.shape), _resident(w.shape)],
        out_specs=blk(x),
        compiler_params=_params(("parallel", "parallel"), 40),
        name="mixer_out_proj",
    )(oa, ob, oc, x, gt, go, w)


def _regroup_w_in(w_in):
    WA, WB, WC = H_A * HEAD_DIM, H_B * HEAD_DIM, H_C * HEAD_DIM
    WI = IDX_HEADS * IDX_DIM
    sizes = [WA] * 3 + [WI, IDX_DIM, IDX_HEADS] + [WB] * 3 + [WC] * 3
    offs = np.concatenate([[0], np.cumsum(sizes)])
    seg = [w_in[:, offs[n]:offs[n + 1]] for n in range(len(sizes))]
    qa, ka, va, qi, ki, wi, qb, kb, vb, qc, kc, vc = seg
    pad = jnp.zeros((w_in.shape[0], LANES - IDX_DIM - IDX_HEADS), w_in.dtype)
    return jnp.concatenate([qa, ka, va, qb, kb, vb, qc, kc, vc, qi, ki, wi, pad], axis=1).astype(bf16)


def kernel(x, c, w_ada, b_ada, norm_ffn1, w_gate1, w_up1, w_down1, norm_mix, w_in, qn_a, kn_a, qn_c, kn_c,
           g_out, w_out, norm_ffn2, w_gate2, w_up2, w_down2):
    B, S, D = x.shape
    L = w_ada.shape[0]
    topk = min(TOPK_MAX, S // 4)
    slopes_a, slopes_c = _alibi_slopes()
    tm = min(512, S)
    assert all(window // dil == TQ for window, dil in DILATIONS)
    wide_dils = tuple(sorted(dil for _, dil in DILATIONS if dil > 1))
    mod = _modulation(c, w_ada, b_ada).reshape(L, B, N_MOD, 1, D)
    tile = lambda g, n: jnp.tile(g, n).reshape(1, -1)
    for l in range(L):
        sh1, sc1, g1, sh2, sc2, g2, sh3, sc3, g3 = [mod[l, :, n] for n in range(N_MOD)]
        x = _ffn(x, sh1, sc1, g1, norm_ffn1[l].reshape(1, D), w_gate1[l].astype(bf16), w_up1[l].astype(bf16),
                 w_down1[l].astype(bf16), tm=tm, n_chunks=2)
        qkv_a, qkv_b, qkv_c, qi, kib, kw, *qkv_cls = _proj(
            x, sh2, sc2, norm_mix[l].reshape(1, D), _regroup_w_in(w_in[l]),
            tile(qn_a[l], H_A), tile(kn_a[l], H_A), tile(qn_c[l], H_C), tile(kn_c[l], H_C), tm=tm, dils=wide_dils)
        o_a = _dsa(qkv_a, qi, kib, kw, topk=topk, slopes=slopes_a, ck=min(512, S))
        o_b = _stick_breaking(qkv_b, kb=256)
        o_c = _dilated({1: qkv_c.reshape(B, 1, S, qkv_c.shape[2]), **dict(zip(wide_dils, qkv_cls))}, slopes_c)
        x = _out_proj(o_a, o_b, o_c, x, g2, g_out[l].reshape(1, -1), w_out[l].astype(bf16), tm=tm)
        x = _ffn(x, sh3, sc3, g3, norm_ffn2[l].reshape(1, D), w_gate2[l].astype(bf16), w_up2[l].astype(bf16),
                 w_down2[l].astype(bf16), tm=tm, n_chunks=2)
    return x
```

```python
import functools

import numpy as np
import jax
import jax.numpy as jnp
from jax import lax
from jax.experimental import pallas as pl
from jax.experimental.pallas import tpu as pltpu

HEAD_DIM = 64
H_A = 4
H_B = 4
H_C = 8
IDX_HEADS = 8
IDX_DIM = 64
TOPK_MAX = 256
DILATIONS = ((128, 1), (512, 4), (2048, 16))
N_MOD = 9
EPS = 1e-6

TQ = 128
LANES = 128
ROW_GROUP = 128
NEG = -1e30
INT_MIN = -2 ** 31
NEG_INF_KEY = -2139095041
EXP_DEAD = -104.0
V7X_VMEM_BYTES = 64 * 1024 * 1024

f32 = jnp.float32
bf16 = jnp.bfloat16
i32 = jnp.int32

_NT = (((1,), (1,)), ((), ()))


def _params(sem, vmem_mb):
    return pltpu.CompilerParams(dimension_semantics=sem,
                                vmem_limit_bytes=min(vmem_mb * 1024 * 1024, V7X_VMEM_BYTES - (4 << 20)))


def _alibi_slopes():
    n = H_A + H_C
    s = np.exp2(-8.0 * np.arange(1, n + 1, dtype=np.float32) / np.float32(n)).astype(np.float32)
    idx = np.arange(n)
    return [float(v) for v in s[idx % 3 == 2]], [float(v) for v in s[idx % 3 != 2]]


def _resident(shape):
    nd = len(shape)
    return pl.BlockSpec(shape, lambda *_: (0,) * nd, pipeline_mode=pl.Buffered(1))


def _mod_kernel(c_ref, w_ref, b_ref, o_ref):
    c = c_ref[...]
    cond = c * jax.nn.sigmoid(c)
    o_ref[0] = jnp.dot(cond.astype(bf16), w_ref[0].astype(bf16), preferred_element_type=f32) + b_ref[0]


def _modulation(c, w_ada, b_ada):
    L, D, ND = w_ada.shape
    B = c.shape[0]
    tn = D
    return pl.pallas_call(
        _mod_kernel,
        out_shape=jax.ShapeDtypeStruct((L, B, ND), f32),
        grid=(L, ND // tn),
        in_specs=[pl.BlockSpec((B, D), lambda l, n: (0, 0)),
                  pl.BlockSpec((1, D, tn), lambda l, n: (l, 0, n)),
                  pl.BlockSpec((1, 1, tn), lambda l, n: (l, 0, n))],
        out_specs=pl.BlockSpec((1, B, tn), lambda l, n: (l, 0, n)),
        compiler_params=_params(("arbitrary", "arbitrary"), 32),
        name="adaln_mod",
    )(c, w_ada, b_ada.reshape(L, 1, ND))


def _norm_mod(x, nw, sc, sh):
    ms = jnp.mean(x * x, axis=-1, keepdims=True)
    y = x * lax.rsqrt(ms + EPS) * nw
    return y * (1.0 + sc) + sh


def _ffn_kernel(x_ref, sh_ref, sc_ref, gt_ref, nw_ref, wg_ref, wu_ref, wd_ref, o_ref, *, n_chunks):
    x = x_ref[0]
    h = _norm_mod(x, nw_ref[...], sc_ref[0], sh_ref[0]).astype(bf16)
    F = wg_ref.shape[1]
    fc = F // n_chunks
    y = None
    for c in range(n_chunks):
        g = jnp.dot(h, wg_ref[:, c * fc:(c + 1) * fc], preferred_element_type=f32)
        u = jnp.dot(h, wu_ref[:, c * fc:(c + 1) * fc], preferred_element_type=f32)
        a = (g * jax.nn.sigmoid(g) * u).astype(bf16)
        part = jnp.dot(a, wd_ref[c * fc:(c + 1) * fc, :], preferred_element_type=f32)
        y = part if y is None else y + part
    o_ref[0] = x + (0.5 * gt_ref[0]) * y


def _ffn(x, sh, sc, gt, nw, wg, wu, wd, *, tm, n_chunks):
    B, S, D = x.shape
    F = wg.shape[1]
    vec = pl.BlockSpec((1, 1, D), lambda b, j: (b, 0, 0))
    return pl.pallas_call(
        functools.partial(_ffn_kernel, n_chunks=n_chunks),
        out_shape=jax.ShapeDtypeStruct((B, S, D), f32),
        grid=(B, S // tm),
        in_specs=[pl.BlockSpec((1, tm, D), lambda b, j: (b, j, 0)), vec, vec, vec,
                  _resident((1, D)), _resident((D, F)), _resident((D, F)), _resident((F, D))],
        out_specs=pl.BlockSpec((1, tm, D), lambda b, j: (b, j, 0)),
        compiler_params=_params(("parallel", "parallel"), 56),
        name="ffn_swiglu",
    )(x, sh, sc, gt, nw, wg, wu, wd)


def _head_rms(x, gain_row, scale):
    W = x.shape[1]
    outs = []
    r = lax.broadcasted_iota(i32, (256, 256), 0) >> 6
    c = lax.broadcasted_iota(i32, (256, 256), 1) >> 6
    bd = jnp.where(r == c, 1.0, 0.0).astype(bf16)
    for s in range(W // 256):
        xs = x[:, s * 256:(s + 1) * 256]
        sq = xs * xs
        hi = sq.astype(bf16)
        lo = (sq - hi.astype(f32)).astype(bf16)
        ss = jnp.dot(hi, bd, preferred_element_type=f32) + jnp.dot(lo, bd, preferred_element_type=f32)
        y = xs * lax.rsqrt(ss * (1.0 / HEAD_DIM) + EPS) * gain_row[:, s * 256:(s + 1) * 256]
        outs.append(y * scale if scale != 1.0 else y)
    return outs[0] if len(outs) == 1 else jnp.concatenate(outs, axis=-1)


def _proj_kernel(x_ref, sh_ref, sc_ref, nw_ref, w_ref, qna_ref, kna_ref, qnc_ref, knc_ref,
                 a_ref, b_ref, c_ref, qi_ref, kib_ref, kw_ref, *rest, dils):
    class_refs, slab_ref = rest[:-1], rest[-1]
    tm = x_ref.shape[1]
    x = x_ref[0]
    h = _norm_mod(x, nw_ref[...], sc_ref[0], sh_ref[0]).astype(bf16)
    WA, WB, WC = H_A * HEAD_DIM, H_B * HEAD_DIM, H_C * HEAD_DIM
    qscale = HEAD_DIM ** -0.5
    o = 0

    def mm(width):
        nonlocal o
        r = jnp.dot(h, w_ref[:, o:o + width], preferred_element_type=f32)
        o += width
        return r

    pa = mm(3 * WA)
    a_ref[0] = jnp.concatenate([_head_rms(pa[:, :WA], qna_ref[...], qscale),
                                _head_rms(pa[:, WA:2 * WA], kna_ref[...], 1.0),
                                pa[:, 2 * WA:]], axis=-1).astype(bf16)
    pb = mm(3 * WB)
    b_ref[0] = jnp.concatenate([pb[:, :WB] * qscale, pb[:, WB:]], axis=-1).astype(bf16)
    pc = mm(3 * WC)
    qkv_c = jnp.concatenate([_head_rms(pc[:, :WC], qnc_ref[...], qscale),
                             _head_rms(pc[:, WC:2 * WC], knc_ref[...], 1.0),
                             pc[:, 2 * WC:]], axis=-1)
    c_ref[0] = qkv_c.astype(bf16)
    n_slabs = 3 * WC // LANES
    for g in range(n_slabs):
        slab_ref[g] = qkv_c[:, g * LANES:(g + 1) * LANES]
    for ref, dil in zip(class_refs, dils):
        for r in range(dil):
            for g in range(n_slabs):
                ref[0, r, :, g * LANES:(g + 1) * LANES] = slab_ref[g, pl.ds(r, tm // dil, stride=dil), :].astype(bf16)
    qi_ref[0] = mm(IDX_HEADS * IDX_DIM).astype(bf16)
    kw = mm(LANES)
    kib_ref[0] = kw.astype(bf16)
    kw_ref[0] = kw


def _proj(x, sh, sc, nw, w2, qna, kna, qnc, knc, *, tm, dils):
    B, S, D = x.shape
    WA, WB, WC = H_A * HEAD_DIM, H_B * HEAD_DIM, H_C * HEAD_DIM
    WI = IDX_HEADS * IDX_DIM
    vec = pl.BlockSpec((1, 1, D), lambda b, j: (b, 0, 0))
    widths = (3 * WA, 3 * WB, 3 * WC, WI, LANES, LANES)
    dtypes = (bf16, bf16, bf16, bf16, bf16, f32)
    out_shape = [jax.ShapeDtypeStruct((B, S, w), dt) for w, dt in zip(widths, dtypes)]
    out_specs = [pl.BlockSpec((1, tm, w), lambda b, j: (b, j, 0)) for w in widths]
    for dil in dils:
        out_shape.append(jax.ShapeDtypeStruct((B, dil, S // dil, 3 * WC), bf16))
        out_specs.append(pl.BlockSpec((1, dil, tm // dil, 3 * WC), lambda b, j: (b, 0, j, 0)))
    return pl.pallas_call(
        functools.partial(_proj_kernel, dils=dils),
        out_shape=out_shape,
        grid=(B, S // tm),
        in_specs=[pl.BlockSpec((1, tm, D), lambda b, j: (b, j, 0)), vec, vec,
                  _resident((1, D)), _resident(w2.shape),
                  _resident((1, WA)), _resident((1, WA)), _resident((1, WC)), _resident((1, WC))],
        out_specs=out_specs,
        scratch_shapes=[pltpu.VMEM((3 * WC // LANES, tm, LANES), f32)],
        compiler_params=_params(("parallel", "parallel"), 56),
        name="mixer_in_proj",
    )(x, sh, sc, nw, w2, qna, kna, qnc, knc)


def _dsa_kernel(q_ref, k_ref, v_ref, qi_ref, ki_ref, w_ref, o_ref,
                sc_ref, bias_ref, logit_ref, mx_ref, sum_ref, acc_ref, opnd_ref, *, topk, slopes, ck):
    TQ = q_ref.shape[1]
    S = k_ref.shape[1]
    i = pl.program_id(1)
    t0 = i * TQ
    nck = lax.div(t0 + TQ + ck - 1, ck)
    lt = ck // LANES
    tpos = t0 + lax.broadcasted_iota(i32, (TQ, 1), 0)
    lane_pos = lax.broadcasted_iota(i32, (1, ck), 1)
    idx_scale = (IDX_DIM ** -0.5) * (IDX_HEADS ** -0.5)

    def fold(m):
        r = m[:, 0:LANES]
        for t in range(1, lt):
            r = r + m[:, t * LANES:(t + 1) * LANES]
        return r

    def fold_max(x):
        r = x[:, 0:LANES]
        for t in range(1, lt):
            r = jnp.maximum(r, x[:, t * LANES:(t + 1) * LANES])
        return r

    def fold_min(x):
        r = x[:, 0:LANES]
        for t in range(1, lt):
            r = jnp.minimum(r, x[:, t * LANES:(t + 1) * LANES])
        return r

    def to_key(x):
        bits = lax.bitcast_convert_type(x, i32)
        return bits ^ ((bits >> 31) & 0x7FFFFFFF)

    def from_key(k):
        return lax.bitcast_convert_type(k ^ ((k >> 31) & 0x7FFFFFFF), f32)

    def count(pred_fn, *cols):
        lane_t = lax.broadcasted_iota(i32, (1, LANES), 1)
        groups = [slice(r0, r0 + ROW_GROUP) for r0 in range(0, TQ, ROW_GROUP)]
        for n, col in enumerate(cols):
            opnd_ref[n] = lax.bitcast_convert_type(jnp.broadcast_to(col, (TQ, LANES)), i32)

        def body(c, accs):
            new = []
            for rs, acc in zip(groups, accs):
                tiles = [lax.bitcast_convert_type(opnd_ref[n, rs, :], col.dtype) for n, col in enumerate(cols)]
                sc = sc_ref[c, rs, :]
                for t in range(lt):
                    m = pred_fn(sc[:, t * LANES:(t + 1) * LANES], c * ck + t * LANES + lane_t, *tiles)
                    acc = acc + jnp.where(m, 1, 0).astype(i32)
                new.append(acc)
            return tuple(new)
        accs = lax.fori_loop(0, nck, body, tuple(jnp.zeros((ROW_GROUP, LANES), i32) for _ in groups))
        acc = accs[0] if len(accs) == 1 else jnp.concatenate(accs, axis=0)
        return jnp.sum(acc.astype(f32), axis=-1, keepdims=True)

    qi = qi_ref[0]
    qstack = jnp.concatenate([qi[:, h * IDX_DIM:(h + 1) * IDX_DIM] for h in range(IDX_HEADS)], axis=0)
    wts = w_ref[0]
    wcols = [wts[:, IDX_DIM + h:IDX_DIM + h + 1] for h in range(IDX_HEADS)]

    def score_body(c, carry):
        c0 = pl.multiple_of(c * ck, ck)
        kic = ki_ref[0, pl.ds(c0, ck), :][:, :IDX_DIM]
        rel = lax.dot_general(qstack, kic, _NT, preferred_element_type=f32)
        score = wcols[0] * jnp.maximum(rel[0:TQ], 0.0)
        for h in range(1, IDX_HEADS):
            score = score + wcols[h] * jnp.maximum(rel[h * TQ:(h + 1) * TQ], 0.0)
        score = score * idx_scale
        score = jnp.where(score == 0.0, 0.0, score)
        sc_ref[c] = jnp.where(c0 + lane_pos <= tpos, score, -jnp.inf)
        return carry

    lax.fori_loop(0, nck, score_body, 0)

    kf = float(topk)

    def bit_body(b, p):
        cand_u = p | lax.shift_left(jnp.int32(1), 31 - b)
        cnt = count(lambda sc, kp, cand: sc >= cand, from_key(cand_u ^ INT_MIN))
        return jnp.where(cnt >= kf, cand_u, p)

    tau_key = lax.fori_loop(0, 32, bit_body, jnp.zeros((TQ, 1), i32)) ^ INT_MIN
    tau = from_key(tau_key)
    tau_eff = from_key(jnp.maximum(tau_key, NEG_INF_KEY + 1))

    cnt_gt = count(lambda sc, kp, t: sc > t, tau)
    cnt_ge = count(lambda sc, kp, t: sc >= t, tau)
    need = (cnt_ge > kf) & (tau_key > NEG_INF_KEY)
    any_need = jnp.max(jnp.where(need, 1.0, 0.0)) > 0.0

    @pl.when(jnp.logical_not(any_need))
    def _():
        def bias_body(c, carry):
            bias_ref[c] = jnp.where(sc_ref[c] >= tau_eff, 0.0, NEG)
            return carry

        lax.fori_loop(0, nck, bias_body, 0)

    @pl.when(any_need)
    def _():
        quota = kf - cnt_gt
        nbits = max(1, (S - 1).bit_length())

        def jbit(b, lo):
            cand = lo | lax.shift_left(jnp.int32(1), nbits - 1 - b)
            c = count(lambda sc, kp, t, cd: (sc == t) & (kp < cd), tau, cand)
            return jnp.where(c < quota, cand, lo)

        jstar = lax.fori_loop(0, nbits, jbit, jnp.zeros((TQ, 1), i32))

        def bias_body(c, carry):
            sc = sc_ref[c]
            kp = c * ck + lane_pos
            tied = (sc > tau) | ((sc == tau) & (kp <= jstar))
            plain = jnp.logical_not(need) & (sc >= tau_eff)
            sel = ((need & tied) | plain) & (kp <= tpos)
            bias_ref[c] = jnp.where(sel, 0.0, NEG)
            return carry

        lax.fori_loop(0, nck, bias_body, 0)

    lane = lax.broadcasted_iota(i32, (1, LANES), 1)
    own = [(lane < HEAD_DIM) if h % 2 == 0 else (lane >= HEAD_DIM) for h in range(H_A)]
    qm = []
    for h in range(H_A):
        qt = q_ref[0, :, (h // 2) * LANES:(h // 2 + 1) * LANES]
        qm.append(jnp.where(own[h], qt, jnp.zeros_like(qt)))

    mx_ref[...] = jnp.full(mx_ref.shape, NEG, f32)

    def logit_body(c, carry):
        c0 = pl.multiple_of(c * ck, ck)
        bias = bias_ref[c]
        pos = (c0 + lane_pos).astype(f32)
        for h in range(H_A):
            kt = k_ref[0, pl.ds(c0, ck), (h // 2) * LANES:(h // 2 + 1) * LANES]
            lg = lax.dot_general(qm[h], kt, _NT, preferred_element_type=f32) + (slopes[h] * pos + bias)
            logit_ref[h, c] = lg
            mx_ref[h] = jnp.maximum(mx_ref[h], fold_max(lg))
        return carry

    lax.fori_loop(0, nck, logit_body, 0)
    row_max = [jnp.max(mx_ref[h], axis=-1, keepdims=True) for h in range(H_A)]
    sum_ref[...] = jnp.zeros(sum_ref.shape, f32)
    acc_ref[...] = jnp.zeros(acc_ref.shape, f32)

    def pv_body(c, carry):
        c0 = pl.multiple_of(c * ck, ck)
        for h in range(H_A):
            g = h // 2
            vt = v_ref[0, pl.ds(c0, ck), g * LANES:(g + 1) * LANES]
            p = jnp.exp(logit_ref[h, c] - row_max[h])
            sum_ref[h] = sum_ref[h] + fold(p)
            acc_ref[g] = acc_ref[g] + jnp.dot(p.astype(bf16), jnp.where(own[h], vt, jnp.zeros_like(vt)),
                                              preferred_element_type=f32)
        return carry

    lax.fori_loop(0, nck, pv_body, 0)
    outs = []
    for g in range(H_A // 2):
        l_e = jnp.sum(sum_ref[2 * g], axis=-1, keepdims=True)
        l_o = jnp.sum(sum_ref[2 * g + 1], axis=-1, keepdims=True)
        outs.append(acc_ref[g] / jnp.where(own[2 * g], l_e, l_o))
    o_ref[0] = jnp.concatenate(outs, axis=-1)


def _dsa(qkv, qi, kib, kw, *, topk, slopes, ck, tq):
    B, S, _ = qkv.shape
    W = H_A * HEAD_DIM
    WI = IDX_HEADS * IDX_DIM
    return pl.pallas_call(
        functools.partial(_dsa_kernel, topk=topk, slopes=slopes, ck=ck),
        out_shape=jax.ShapeDtypeStruct((B, S, W), f32),
        grid=(B, S // tq),
        in_specs=[pl.BlockSpec((1, tq, W), lambda b, i: (b, i, 0)),
                  pl.BlockSpec((1, S, W), lambda b, i: (b, 0, 1)),
                  pl.BlockSpec((1, S, W), lambda b, i: (b, 0, 2)),
                  pl.BlockSpec((1, tq, WI), lambda b, i: (b, i, 0)),
                  pl.BlockSpec((1, S, LANES), lambda b, i: (b, 0, 0)),
                  pl.BlockSpec((1, tq, LANES), lambda b, i: (b, i, 0))],
        out_specs=pl.BlockSpec((1, tq, W), lambda b, i: (b, i, 0)),
        scratch_shapes=[pltpu.VMEM((S // ck, tq, ck), f32),
                        pltpu.VMEM((S // ck, tq, ck), f32),
                        pltpu.VMEM((H_A, S // ck, tq, ck), f32),
                        pltpu.VMEM((H_A, tq, LANES), f32),
                        pltpu.VMEM((H_A, tq, LANES), f32),
                        pltpu.VMEM((H_A // 2, tq, LANES), f32),
                        pltpu.VMEM((2, tq, LANES), i32)],
        compiler_params=_params(("parallel", "arbitrary"), 56),
        name="dsa_attention",
    )(qkv, qkv, qkv, qi, kib, kw)


def _sb_kernel(q_ref, k_ref, v_ref, o_ref, *, kb):
    i = pl.program_id(1)
    t0 = i * TQ
    nkb = lax.div(t0 + TQ + kb - 1, kb)
    tpos = t0 + lax.broadcasted_iota(i32, (TQ, 1), 0)
    lane_pos = lax.broadcasted_iota(i32, (1, kb), 1)
    tri = jnp.where(lax.broadcasted_iota(i32, (kb, kb), 0) > lax.broadcasted_iota(i32, (kb, kb), 1),
                    1.0, 0.0).astype(bf16)
    q = q_ref[0]
    qs = [q[:, h * HEAD_DIM:(h + 1) * HEAD_DIM] for h in range(H_B)]

    def cond(carry):
        jj, alive = carry[0], carry[1]
        return (jj < nkb) & (alive > EXP_DEAD)

    def body(carry):
        jj, _, runs, accs = carry
        s0 = pl.multiple_of((nkb - 1 - jj) * kb, kb)
        kt = k_ref[0, pl.ds(s0, kb), :]
        vt = v_ref[0, pl.ds(s0, kb), :]
        strict = (s0 + lane_pos) < tpos
        zs, sps, lms, parts = [], [], [], ([], [], [])
        for h in range(H_B):
            hs = slice(h * HEAD_DIM, (h + 1) * HEAD_DIM)
            z = lax.dot_general(qs[h], kt[:, hs], _NT, preferred_element_type=f32)
            sp = jnp.maximum(z, 0.0) + jnp.log1p(jnp.exp(-jnp.abs(z)))
            lm = jnp.where(strict, -sp, 0.0)
            l1 = lm.astype(bf16)
            r1 = lm - l1.astype(f32)
            l2 = r1.astype(bf16)
            l3 = (r1 - l2.astype(f32)).astype(bf16)
            zs.append(z); sps.append(sp); lms.append(lm)
            parts[0].append(l1); parts[1].append(l2); parts[2].append(l3)
        later = sum(jnp.dot(jnp.concatenate(p, axis=0), tri, preferred_element_type=f32) for p in parts)
        new_runs, new_accs = [], []
        alive = None
        for h in range(H_B):
            hs = slice(h * HEAD_DIM, (h + 1) * HEAD_DIM)
            a = jnp.where(strict, jnp.exp((zs[h] - sps[h]) + (runs[h] + later[h * TQ:(h + 1) * TQ])), 0.0)
            new_accs.append(accs[h] + jnp.dot(a.astype(bf16), vt[:, hs], preferred_element_type=f32))
            run = runs[h] + jnp.sum(lms[h], axis=-1, keepdims=True)
            new_runs.append(run)
            alive = run if alive is None else jnp.maximum(alive, run)
        return jj + 1, jnp.max(alive), tuple(new_runs), tuple(new_accs)

    init = (jnp.int32(0), jnp.float32(0.0),
            tuple(jnp.zeros((TQ, 1), f32) for _ in range(H_B)),
            tuple(jnp.zeros((TQ, HEAD_DIM), f32) for _ in range(H_B)))
    _, _, _, accs = lax.while_loop(cond, body, init)
    o_ref[0] = jnp.concatenate(accs, axis=-1)


def _stick_breaking(qkv, *, kb):
    B, S, _ = qkv.shape
    W = H_B * HEAD_DIM
    return pl.pallas_call(
        functools.partial(_sb_kernel, kb=kb),
        out_shape=jax.ShapeDtypeStruct((B, S, W), f32),
        grid=(B, S // TQ),
        in_specs=[pl.BlockSpec((1, TQ, W), lambda b, i: (b, i, 0)),
                  pl.BlockSpec((1, S, W), lambda b, i: (b, 0, 1)),
                  pl.BlockSpec((1, S, W), lambda b, i: (b, 0, 2))],
        out_specs=pl.BlockSpec((1, TQ, W), lambda b, i: (b, i, 0)),
        compiler_params=_params(("parallel", "arbitrary"), 32),
        name="stick_breaking",
    )(qkv, qkv, qkv)


def _dil_kernel(*refs, dil, slopes, first, last):
    bias_ref, lg_ref, pv_ref = refs[-3:]
    if first:
        q_ref, kp_ref, kc_ref, vp_ref, vc_ref, acc_out, st_out = refs[:-3]
    elif last:
        q_ref, kp_ref, kc_ref, vp_ref, vc_ref, acc_in, st_in, o_ref = refs[:-3]
    else:
        q_ref, kp_ref, kc_ref, vp_ref, vc_ref, acc_in, st_in, acc_out, st_out = refs[:-3]
    j = pl.program_id(1)
    r = pl.program_id(2)
    rows = pl.ds(r, TQ, stride=dil) if dil > 1 else slice(None)

    @pl.when((pl.program_id(0) == 0) & (j == 0) & (r == 0))
    def _():
        u = lax.broadcasted_iota(i32, (TQ, 1), 0)
        c = lax.broadcasted_iota(i32, (1, TQ), 1)
        d_cur = u - c
        d_prev = d_cur + TQ
        for h in range(H_C):
            bias_ref[h, 0] = jnp.where(d_prev <= TQ, d_prev.astype(f32) * (-slopes[h] * dil), NEG)
            bias_ref[h, 1] = jnp.where(d_cur >= 0, d_cur.astype(f32) * (-slopes[h] * dil), NEG)

    pen = jnp.where(j > 0, 0.0, NEG)
    lane = lax.broadcasted_iota(i32, (TQ, LANES), 1)
    lo_half = lane < HEAD_DIM
    owns = (lo_half, jnp.logical_not(lo_half))

    if not first:
        st = st_in[0, rows, :]

    ms = []
    for h in range(H_C):
        ts = slice((h // 2) * LANES, (h // 2 + 1) * LANES)
        qt = q_ref[0, 0, :, ts]
        qm = jnp.where(owns[h % 2], qt, jnp.zeros_like(qt))
        lg_p = lax.dot_general(qm, kp_ref[0, 0, :, ts], _NT, preferred_element_type=f32) + (bias_ref[h, 0] + pen)
        lg_c = lax.dot_general(qm, kc_ref[0, 0, :, ts], _NT, preferred_element_type=f32) + bias_ref[h, 1]
        lg_ref[h, 0] = lg_p
        lg_ref[h, 1] = lg_c
        m = jnp.max(jnp.maximum(lg_p, lg_c), axis=-1, keepdims=True)
        if not first:
            m = jnp.maximum(m, st[:, h:h + 1])
        ms.append(m)

    for h in range(H_C):
        ts = slice((h // 2) * LANES, (h // 2 + 1) * LANES)
        vpt, vct = vp_ref[0, 0, :, ts], vc_ref[0, 0, :, ts]
        one = jnp.ones_like(vpt)
        p_p = jnp.exp(lg_ref[h, 0] - ms[h]).astype(bf16)
        p_c = jnp.exp(lg_ref[h, 1] - ms[h]).astype(bf16)
        pv_ref[h] = (jnp.dot(p_p, jnp.where(owns[h % 2], vpt, one), preferred_element_type=f32)
                     + jnp.dot(p_c, jnp.where(owns[h % 2], vct, one), preferred_element_type=f32))

    st_new = jnp.zeros((TQ, LANES), f32)
    for g in range(H_C // 2):
        ts = slice(g * LANES, (g + 1) * LANES)
        pv0, pv1 = pv_ref[2 * g], pv_ref[2 * g + 1]
        acc_t = jnp.where(lo_half, pv0, pv1)
        l_t = pltpu.roll(jnp.where(lo_half, pv1, pv0), HEAD_DIM, 1)
        if not first:
            alpha = jnp.exp(jnp.where(lo_half, st[:, 2 * g:2 * g + 1] - ms[2 * g],
                                      st[:, 2 * g + 1:2 * g + 2] - ms[2 * g + 1]))
            l_old = jnp.where(lo_half, st[:, H_C + 2 * g:H_C + 2 * g + 1], st[:, H_C + 2 * g + 1:H_C + 2 * g + 2])
            acc_t = acc_t + alpha * acc_in[0, g, rows, :]
            l_t = l_t + alpha * l_old
        if last:
            o_ref[0, :, ts] = acc_t / l_t
        else:
            acc_out[0, g, rows, :] = acc_t
            for e in range(2):
                h = 2 * g + e
                st_new = jnp.where(lane == h, ms[h], st_new)
                st_new = jnp.where(lane == H_C + h, l_t[:, e * HEAD_DIM:e * HEAD_DIM + 1], st_new)
    if not last:
        st_out[0, rows, :] = st_new


def _dilated_branch(qkv_cls, acc, st, *, slopes, first, last):
    B, dil, Lc, W3 = qkv_cls.shape
    W = W3 // 3
    S = dil * Lc
    nj = Lc // TQ
    n_slabs = W // LANES
    assert last == (dil == 1)
    blk = lambda comp, prev: pl.BlockSpec(
        (1, 1, TQ, W), (lambda b, j, r: (b, r, jnp.maximum(j - 1, 0), comp)) if prev
        else (lambda b, j, r: (b, r, j, comp)))
    acc_spec = pl.BlockSpec((1, n_slabs, dil * TQ, LANES), lambda b, j, r: (b, 0, j, 0))
    st_spec = pl.BlockSpec((1, dil * TQ, LANES), lambda b, j, r: (b, j, 0))
    in_specs = [blk(0, False), blk(1, True), blk(1, False), blk(2, True), blk(2, False)]
    args = [qkv_cls] * 5
    if not first:
        in_specs += [acc_spec, st_spec]
        args += [acc, st]
    if last:
        out_shape = jax.ShapeDtypeStruct((B, S, W), f32)
        out_specs = pl.BlockSpec((1, TQ, W), lambda b, j, r: (b, j, 0))
    else:
        out_shape = [jax.ShapeDtypeStruct((B, n_slabs, S, LANES), f32), jax.ShapeDtypeStruct((B, S, LANES), f32)]
        out_specs = [acc_spec, st_spec]
    return pl.pallas_call(
        functools.partial(_dil_kernel, dil=dil, slopes=slopes, first=first, last=last),
        out_shape=out_shape,
        grid=(B, nj, dil),
        in_specs=in_specs,
        out_specs=out_specs,
        scratch_shapes=[pltpu.VMEM((H_C, 2, TQ, TQ), f32),
                        pltpu.VMEM((H_C, 2, TQ, TQ), f32),
                        pltpu.VMEM((H_C, TQ, LANES), f32)],
        compiler_params=_params(("arbitrary", "arbitrary", "arbitrary"), 40),
        name=f"dilated_d{dil}",
    )(*args)


def _dilated(qkv_by_dil, slopes):
    dils = sorted(qkv_by_dil, reverse=True)
    acc = st = None
    for n, dil in enumerate(dils):
        first, last = n == 0, n == len(dils) - 1
        res = _dilated_branch(qkv_by_dil[dil], acc, st, slopes=slopes, first=first, last=last)
        if last:
            return res
        acc, st = res


def _out_kernel(oa_ref, ob_ref, oc_ref, x_ref, gt_ref, go_ref, w_ref, o_ref):
    go = go_ref[...]
    y = None
    o = 0
    for ref in (oa_ref, ob_ref, oc_ref):
        v = ref[0]
        w = v.shape[1]
        ms = jnp.mean(v * v, axis=-1, keepdims=True)
        yn = (v * lax.rsqrt(ms + EPS) * go[:, o:o + w]).astype(bf16)
        part = jnp.dot(yn, w_ref[o:o + w, :], preferred_element_type=f32)
        y = part if y is None else y + part
        o += w
    o_ref[0] = x_ref[0] + gt_ref[0] * y


def _out_proj(oa, ob, oc, x, gt, go, w, *, tm):
    B, S, D = x.shape
    blk = lambda a: pl.BlockSpec((1, tm, a.shape[2]), lambda b, j: (b, j, 0))
    return pl.pallas_call(
        _out_kernel,
        out_shape=jax.ShapeDtypeStruct((B, S, D), f32),
        grid=(B, S // tm),
        in_specs=[blk(oa), blk(ob), blk(oc), blk(x), pl.BlockSpec((1, 1, D), lambda b, j: (b, 0, 0)),
                  _resident(go.shape), _resident(w.shape)],
        out_specs=blk(x),
        compiler_params=_params(("parallel", "parallel"), 40),
        name="mixer_out_proj",
    )(oa, ob, oc, x, gt, go, w)


def _regroup_w_in(w_in):
    WA, WB, WC = H_A * HEAD_DIM, H_B * HEAD_DIM, H_C * HEAD_DIM
    WI = IDX_HEADS * IDX_DIM
    sizes = [WA] * 3 + [WI, IDX_DIM, IDX_HEADS] + [WB] * 3 + [WC] * 3
    offs = np.concatenate([[0], np.cumsum(sizes)])
    seg = [w_in[:, offs[n]:offs[n + 1]] for n in range(len(sizes))]
    qa, ka, va, qi, ki, wi, qb, kb, vb, qc, kc, vc = seg
    pad = jnp.zeros((w_in.shape[0], LANES - IDX_DIM - IDX_HEADS), w_in.dtype)
    return jnp.concatenate([qa, ka, va, qb, kb, vb, qc, kc, vc, qi, ki, wi, pad], axis=1).astype(bf16)


def kernel(x, c, w_ada, b_ada, norm_ffn1, w_gate1, w_up1, w_down1, norm_mix, w_in, qn_a, kn_a, qn_c, kn_c,
           g_out, w_out, norm_ffn2, w_gate2, w_up2, w_down2):
    B, S, D = x.shape
    L = w_ada.shape[0]
    topk = min(TOPK_MAX, S // 4)
    slopes_a, slopes_c = _alibi_slopes()
    tm = min(512, S)
    assert all(window // dil == TQ for window, dil in DILATIONS)
    wide_dils = tuple(sorted(dil for _, dil in DILATIONS if dil > 1))
    mod = _modulation(c, w_ada, b_ada).reshape(L, B, N_MOD, 1, D)
    tile = lambda g, n: jnp.tile(g, n).reshape(1, -1)
    for l in range(L):
        sh1, sc1, g1, sh2, sc2, g2, sh3, sc3, g3 = [mod[l, :, n] for n in range(N_MOD)]
        x = _ffn(x, sh1, sc1, g1, norm_ffn1[l].reshape(1, D), w_gate1[l].astype(bf16), w_up1[l].astype(bf16),
                 w_down1[l].astype(bf16), tm=tm, n_chunks=2)
        qkv_a, qkv_b, qkv_c, qi, kib, kw, *qkv_cls = _proj(
            x, sh2, sc2, norm_mix[l].reshape(1, D), _regroup_w_in(w_in[l]),
            tile(qn_a[l], H_A), tile(kn_a[l], H_A), tile(qn_c[l], H_C), tile(kn_c[l], H_C), tm=tm, dils=wide_dils)
        o_a = _dsa(qkv_a, qi, kib, kw, topk=topk, slopes=slopes_a, ck=min(512, S), tq=min(256, S))
        o_b = _stick_breaking(qkv_b, kb=256)
        o_c = _dilated({1: qkv_c.reshape(B, 1, S, qkv_c.shape[2]), **dict(zip(wide_dils, qkv_cls))}, slopes_c)
        x = _out_proj(o_a, o_b, o_c, x, g2, g_out[l].reshape(1, -1), w_out[l].astype(bf16), tm=tm)
        x = _ffn(x, sh3, sc3, g3, norm_ffn2[l].reshape(1, D), w_gate2[l].astype(bf16), w_up2[l].astype(bf16),
                 w_down2[l].astype(bf16), tm=tm, n_chunks=2)
    return x
```

```python
import functools

import numpy as np
import jax
import jax.numpy as jnp
from jax import lax
from jax.experimental import pallas as pl
from jax.experimental.pallas import tpu as pltpu

HEAD_DIM = 64
H_A = 4
H_B = 4
H_C = 8
IDX_HEADS = 8
IDX_DIM = 64
TOPK_MAX = 256
DILATIONS = ((128, 1), (512, 4), (2048, 16))
N_MOD = 9
EPS = 1e-6

TQ = 128
LANES = 128
ROW_GROUP = 128
NEG = -1e30
INT_MIN = -2 ** 31
HI16_MASK = -65536
NEG_INF_KEY = -2139095041
EXP_DEAD = -104.0
V7X_VMEM_BYTES = 64 * 1024 * 1024

f32 = jnp.float32
bf16 = jnp.bfloat16
i32 = jnp.int32

_NT = (((1,), (1,)), ((), ()))


def _params(sem, vmem_mb):
    return pltpu.CompilerParams(dimension_semantics=sem,
                                vmem_limit_bytes=min(vmem_mb * 1024 * 1024, V7X_VMEM_BYTES - (4 << 20)))


def _alibi_slopes():
    n = H_A + H_C
    s = np.exp2(-8.0 * np.arange(1, n + 1, dtype=np.float32) / np.float32(n)).astype(np.float32)
    idx = np.arange(n)
    return [float(v) for v in s[idx % 3 == 2]], [float(v) for v in s[idx % 3 != 2]]


def _resident(shape):
    nd = len(shape)
    return pl.BlockSpec(shape, lambda *_: (0,) * nd, pipeline_mode=pl.Buffered(1))


def _mod_kernel(c_ref, w_ref, b_ref, o_ref):
    c = c_ref[...]
    cond = c * jax.nn.sigmoid(c)
    o_ref[0] = jnp.dot(cond.astype(bf16), w_ref[0].astype(bf16), preferred_element_type=f32) + b_ref[0]


def _modulation(c, w_ada, b_ada):
    L, D, ND = w_ada.shape
    B = c.shape[0]
    tn = D
    return pl.pallas_call(
        _mod_kernel,
        out_shape=jax.ShapeDtypeStruct((L, B, ND), f32),
        grid=(L, ND // tn),
        in_specs=[pl.BlockSpec((B, D), lambda l, n: (0, 0)),
                  pl.BlockSpec((1, D, tn), lambda l, n: (l, 0, n)),
                  pl.BlockSpec((1, 1, tn), lambda l, n: (l, 0, n))],
        out_specs=pl.BlockSpec((1, B, tn), lambda l, n: (l, 0, n)),
        compiler_params=_params(("arbitrary", "arbitrary"), 32),
        name="adaln_mod",
    )(c, w_ada, b_ada.reshape(L, 1, ND))


def _norm_mod(x, nw, sc, sh):
    ms = jnp.mean(x * x, axis=-1, keepdims=True)
    y = x * lax.rsqrt(ms + EPS) * nw
    return y * (1.0 + sc) + sh


def _ffn_kernel(x_ref, sh_ref, sc_ref, gt_ref, nw_ref, wg_ref, wu_ref, wd_ref, o_ref, *, n_chunks):
    x = x_ref[0]
    h = _norm_mod(x, nw_ref[...], sc_ref[0], sh_ref[0]).astype(bf16)
    F = wg_ref.shape[1]
    fc = F // n_chunks
    y = None
    for c in range(n_chunks):
        g = jnp.dot(h, wg_ref[:, c * fc:(c + 1) * fc], preferred_element_type=f32)
        u = jnp.dot(h, wu_ref[:, c * fc:(c + 1) * fc], preferred_element_type=f32)
        a = (g * jax.nn.sigmoid(g) * u).astype(bf16)
        part = jnp.dot(a, wd_ref[c * fc:(c + 1) * fc, :], preferred_element_type=f32)
        y = part if y is None else y + part
    o_ref[0] = x + (0.5 * gt_ref[0]) * y


def _ffn(x, sh, sc, gt, nw, wg, wu, wd, *, tm, n_chunks):
    B, S, D = x.shape
    F = wg.shape[1]
    vec = pl.BlockSpec((1, 1, D), lambda b, j: (b, 0, 0))
    return pl.pallas_call(
        functools.partial(_ffn_kernel, n_chunks=n_chunks),
        out_shape=jax.ShapeDtypeStruct((B, S, D), f32),
        grid=(B, S // tm),
        in_specs=[pl.BlockSpec((1, tm, D), lambda b, j: (b, j, 0)), vec, vec, vec,
                  _resident((1, D)), _resident((D, F)), _resident((D, F)), _resident((F, D))],
        out_specs=pl.BlockSpec((1, tm, D), lambda b, j: (b, j, 0)),
        compiler_params=_params(("parallel", "parallel"), 56),
        name="ffn_swiglu",
    )(x, sh, sc, gt, nw, wg, wu, wd)


def _head_rms(x, gain_row, scale):
    W = x.shape[1]
    outs = []
    r = lax.broadcasted_iota(i32, (256, 256), 0) >> 6
    c = lax.broadcasted_iota(i32, (256, 256), 1) >> 6
    bd = jnp.where(r == c, 1.0, 0.0).astype(bf16)
    for s in range(W // 256):
        xs = x[:, s * 256:(s + 1) * 256]
        sq = xs * xs
        hi = sq.astype(bf16)
        lo = (sq - hi.astype(f32)).astype(bf16)
        ss = jnp.dot(hi, bd, preferred_element_type=f32) + jnp.dot(lo, bd, preferred_element_type=f32)
        y = xs * lax.rsqrt(ss * (1.0 / HEAD_DIM) + EPS) * gain_row[:, s * 256:(s + 1) * 256]
        outs.append(y * scale if scale != 1.0 else y)
    return outs[0] if len(outs) == 1 else jnp.concatenate(outs, axis=-1)


def _proj_kernel(x_ref, sh_ref, sc_ref, nw_ref, w_ref, qna_ref, kna_ref, qnc_ref, knc_ref,
                 a_ref, b_ref, c_ref, qi_ref, kib_ref, kw_ref, *rest, dils):
    class_refs, slab_ref = rest[:-1], rest[-1]
    tm = x_ref.shape[1]
    x = x_ref[0]
    h = _norm_mod(x, nw_ref[...], sc_ref[0], sh_ref[0]).astype(bf16)
    WA, WB, WC = H_A * HEAD_DIM, H_B * HEAD_DIM, H_C * HEAD_DIM
    qscale = HEAD_DIM ** -0.5
    o = 0

    def mm(width):
        nonlocal o
        r = jnp.dot(h, w_ref[:, o:o + width], preferred_element_type=f32)
        o += width
        return r

    pa = mm(3 * WA)
    a_ref[0] = jnp.concatenate([_head_rms(pa[:, :WA], qna_ref[...], qscale),
                                _head_rms(pa[:, WA:2 * WA], kna_ref[...], 1.0),
                                pa[:, 2 * WA:]], axis=-1).astype(bf16)
    pb = mm(3 * WB)
    b_ref[0] = jnp.concatenate([pb[:, :WB] * qscale, pb[:, WB:]], axis=-1).astype(bf16)
    pc = mm(3 * WC)
    qkv_c = jnp.concatenate([_head_rms(pc[:, :WC], qnc_ref[...], qscale),
                             _head_rms(pc[:, WC:2 * WC], knc_ref[...], 1.0),
                             pc[:, 2 * WC:]], axis=-1)
    c_ref[0] = qkv_c.astype(bf16)
    n_slabs = 3 * WC // LANES
    for g in range(n_slabs):
        slab_ref[g] = qkv_c[:, g * LANES:(g + 1) * LANES]
    for ref, dil in zip(class_refs, dils):
        for r in range(dil):
            for g in range(n_slabs):
                ref[0, r, :, g * LANES:(g + 1) * LANES] = slab_ref[g, pl.ds(r, tm // dil, stride=dil), :].astype(bf16)
    qi_ref[0] = mm(IDX_HEADS * IDX_DIM).astype(bf16)
    kw = mm(LANES)
    kib_ref[0] = kw.astype(bf16)
    kw_ref[0] = kw


def _proj(x, sh, sc, nw, w2, qna, kna, qnc, knc, *, tm, dils):
    B, S, D = x.shape
    WA, WB, WC = H_A * HEAD_DIM, H_B * HEAD_DIM, H_C * HEAD_DIM
    WI = IDX_HEADS * IDX_DIM
    vec = pl.BlockSpec((1, 1, D), lambda b, j: (b, 0, 0))
    widths = (3 * WA, 3 * WB, 3 * WC, WI, LANES, LANES)
    dtypes = (bf16, bf16, bf16, bf16, bf16, f32)
    out_shape = [jax.ShapeDtypeStruct((B, S, w), dt) for w, dt in zip(widths, dtypes)]
    out_specs = [pl.BlockSpec((1, tm, w), lambda b, j: (b, j, 0)) for w in widths]
    for dil in dils:
        out_shape.append(jax.ShapeDtypeStruct((B, dil, S // dil, 3 * WC), bf16))
        out_specs.append(pl.BlockSpec((1, dil, tm // dil, 3 * WC), lambda b, j: (b, 0, j, 0)))
    return pl.pallas_call(
        functools.partial(_proj_kernel, dils=dils),
        out_shape=out_shape,
        grid=(B, S // tm),
        in_specs=[pl.BlockSpec((1, tm, D), lambda b, j: (b, j, 0)), vec, vec,
                  _resident((1, D)), _resident(w2.shape),
                  _resident((1, WA)), _resident((1, WA)), _resident((1, WC)), _resident((1, WC))],
        out_specs=out_specs,
        scratch_shapes=[pltpu.VMEM((3 * WC // LANES, tm, LANES), f32)],
        compiler_params=_params(("parallel", "parallel"), 56),
        name="mixer_in_proj",
    )(x, sh, sc, nw, w2, qna, kna, qnc, knc)


def _dsa_kernel(q_ref, k_ref, v_ref, qi_ref, ki_ref, w_ref, o_ref,
                sc_ref, bias_ref, logit_ref, mx_ref, sum_ref, acc_ref, opnd_ref, hb_ref, opnd_hi_ref,
                *, topk, slopes, ck):
    TQ = q_ref.shape[1]
    S = k_ref.shape[1]
    i = pl.program_id(1)
    t0 = i * TQ
    nck = lax.div(t0 + TQ + ck - 1, ck)
    lt = ck // LANES
    tpos = t0 + lax.broadcasted_iota(i32, (TQ, 1), 0)
    lane_pos = lax.broadcasted_iota(i32, (1, ck), 1)
    idx_scale = (IDX_DIM ** -0.5) * (IDX_HEADS ** -0.5)

    def fold(m):
        r = m[:, 0:LANES]
        for t in range(1, lt):
            r = r + m[:, t * LANES:(t + 1) * LANES]
        return r

    def fold_max(x):
        r = x[:, 0:LANES]
        for t in range(1, lt):
            r = jnp.maximum(r, x[:, t * LANES:(t + 1) * LANES])
        return r

    def fold_min(x):
        r = x[:, 0:LANES]
        for t in range(1, lt):
            r = jnp.minimum(r, x[:, t * LANES:(t + 1) * LANES])
        return r

    def to_key(x):
        bits = lax.bitcast_convert_type(x, i32)
        return bits ^ ((bits >> 31) & 0x7FFFFFFF)

    def from_key(k):
        return lax.bitcast_convert_type(k ^ ((k >> 31) & 0x7FFFFFFF), f32)

    def count(pred_fn, *cols):
        lane_t = lax.broadcasted_iota(i32, (1, LANES), 1)
        groups = [slice(r0, r0 + ROW_GROUP) for r0 in range(0, TQ, ROW_GROUP)]
        for n, col in enumerate(cols):
            opnd_ref[n] = lax.bitcast_convert_type(jnp.broadcast_to(col, (TQ, LANES)), i32)

        def body(c, accs):
            new = []
            for rs, acc in zip(groups, accs):
                tiles = [lax.bitcast_convert_type(opnd_ref[n, rs, :], col.dtype) for n, col in enumerate(cols)]
                sc = sc_ref[c, rs, :]
                for t in range(lt):
                    m = pred_fn(sc[:, t * LANES:(t + 1) * LANES], c * ck + t * LANES + lane_t, *tiles)
                    acc = acc + jnp.where(m, 1, 0).astype(i32)
                new.append(acc)
            return tuple(new)
        accs = lax.fori_loop(0, nck, body, tuple(jnp.zeros((ROW_GROUP, LANES), i32) for _ in groups))
        acc = accs[0] if len(accs) == 1 else jnp.concatenate(accs, axis=0)
        return jnp.sum(acc.astype(f32), axis=-1, keepdims=True)

    qi = qi_ref[0]
    qstack = jnp.concatenate([qi[:, h * IDX_DIM:(h + 1) * IDX_DIM] for h in range(IDX_HEADS)], axis=0)
    wts = w_ref[0]
    wcols = [wts[:, IDX_DIM + h:IDX_DIM + h + 1] for h in range(IDX_HEADS)]

    def score_body(c, carry):
        c0 = pl.multiple_of(c * ck, ck)
        kic = ki_ref[0, pl.ds(c0, ck), :][:, :IDX_DIM]
        rel = lax.dot_general(qstack, kic, _NT, preferred_element_type=f32)
        score = wcols[0] * jnp.maximum(rel[0:TQ], 0.0)
        for h in range(1, IDX_HEADS):
            score = score + wcols[h] * jnp.maximum(rel[h * TQ:(h + 1) * TQ], 0.0)
        score = score * idx_scale
        score = jnp.where(score == 0.0, 0.0, score)
        score = jnp.where(c0 + lane_pos <= tpos, score, -jnp.inf)
        sc_ref[c] = score
        top = lax.bitcast_convert_type(score, i32) & HI16_MASK
        hb_ref[c] = lax.bitcast_convert_type(top, f32).astype(bf16)
        return carry

    lax.fori_loop(0, nck, score_body, 0)

    def count_hi(cand_col):
        groups = [slice(r0, r0 + ROW_GROUP) for r0 in range(0, TQ, ROW_GROUP)]
        opnd_hi_ref[...] = jnp.broadcast_to(cand_col, (TQ, LANES)).astype(bf16)
        one, zero = jnp.ones((), bf16), jnp.zeros((), bf16)

        def body(c, accs):
            new = []
            for rs, acc in zip(groups, accs):
                cand = opnd_hi_ref[rs, :]
                hb = hb_ref[c, rs, :]
                for t in range(lt):
                    acc = acc + jnp.where(hb[:, t * LANES:(t + 1) * LANES] >= cand, one, zero)
                new.append(acc)
            return tuple(new)
        accs = lax.fori_loop(0, nck, body, tuple(jnp.zeros((ROW_GROUP, LANES), bf16) for _ in groups))
        acc = accs[0] if len(accs) == 1 else jnp.concatenate(accs, axis=0)
        return jnp.sum(acc.astype(f32), axis=-1, keepdims=True)

    kf = float(topk)

    def hi_bit_body(b, p):
        cand_u = p | lax.shift_left(jnp.int32(1), 31 - b)
        cand_bits = lax.bitcast_convert_type(from_key(cand_u ^ INT_MIN), i32) & HI16_MASK
        cnt = count_hi(lax.bitcast_convert_type(cand_bits, f32))
        return jnp.where(cnt >= kf, cand_u, p)

    def bit_body(b, p):
        cand_u = p | lax.shift_left(jnp.int32(1), 31 - b)
        cnt = count(lambda sc, kp, cand: sc >= cand, from_key(cand_u ^ INT_MIN))
        return jnp.where(cnt >= kf, cand_u, p)

    p_hi = lax.fori_loop(0, 16, hi_bit_body, jnp.zeros((TQ, 1), i32))
    tau_key = lax.fori_loop(16, 32, bit_body, p_hi) ^ INT_MIN
    tau = from_key(tau_key)
    tau_eff = from_key(jnp.maximum(tau_key, NEG_INF_KEY + 1))

    cnt_gt = count(lambda sc, kp, t: sc > t, tau)
    cnt_ge = count(lambda sc, kp, t: sc >= t, tau)
    need = (cnt_ge > kf) & (tau_key > NEG_INF_KEY)
    any_need = jnp.max(jnp.where(need, 1.0, 0.0)) > 0.0

    @pl.when(jnp.logical_not(any_need))
    def _():
        def bias_body(c, carry):
            bias_ref[c] = jnp.where(sc_ref[c] >= tau_eff, 0.0, NEG)
            return carry

        lax.fori_loop(0, nck, bias_body, 0)

    @pl.when(any_need)
    def _():
        quota = kf - cnt_gt
        nbits = max(1, (S - 1).bit_length())

        def jbit(b, lo):
            cand = lo | lax.shift_left(jnp.int32(1), nbits - 1 - b)
            c = count(lambda sc, kp, t, cd: (sc == t) & (kp < cd), tau, cand)
            return jnp.where(c < quota, cand, lo)

        jstar = lax.fori_loop(0, nbits, jbit, jnp.zeros((TQ, 1), i32))

        def bias_body(c, carry):
            sc = sc_ref[c]
            kp = c * ck + lane_pos
            tied = (sc > tau) | ((sc == tau) & (kp <= jstar))
            plain = jnp.logical_not(need) & (sc >= tau_eff)
            sel = ((need & tied) | plain) & (kp <= tpos)
            bias_ref[c] = jnp.where(sel, 0.0, NEG)
            return carry

        lax.fori_loop(0, nck, bias_body, 0)

    lane = lax.broadcasted_iota(i32, (1, LANES), 1)
    own = [(lane < HEAD_DIM) if h % 2 == 0 else (lane >= HEAD_DIM) for h in range(H_A)]
    qm = []
    for h in range(H_A):
        qt = q_ref[0, :, (h // 2) * LANES:(h // 2 + 1) * LANES]
        qm.append(jnp.where(own[h], qt, jnp.zeros_like(qt)))

    mx_ref[...] = jnp.full(mx_ref.shape, NEG, f32)

    def logit_body(c, carry):
        c0 = pl.multiple_of(c * ck, ck)
        bias = bias_ref[c]
        pos = (c0 + lane_pos).astype(f32)
        for h in range(H_A):
            kt = k_ref[0, pl.ds(c0, ck), (h // 2) * LANES:(h // 2 + 1) * LANES]
            lg = lax.dot_general(qm[h], kt, _NT, preferred_element_type=f32) + (slopes[h] * pos + bias)
            logit_ref[h, c] = lg
            mx_ref[h] = jnp.maximum(mx_ref[h], fold_max(lg))
        return carry

    lax.fori_loop(0, nck, logit_body, 0)
    row_max = [jnp.max(mx_ref[h], axis=-1, keepdims=True) for h in range(H_A)]
    sum_ref[...] = jnp.zeros(sum_ref.shape, f32)
    acc_ref[...] = jnp.zeros(acc_ref.shape, f32)

    def pv_body(c, carry):
        c0 = pl.multiple_of(c * ck, ck)
        for h in range(H_A):
            g = h // 2
            vt = v_ref[0, pl.ds(c0, ck), g * LANES:(g + 1) * LANES]
            p = jnp.exp(logit_ref[h, c] - row_max[h])
            sum_ref[h] = sum_ref[h] + fold(p)
            acc_ref[g] = acc_ref[g] + jnp.dot(p.astype(bf16), jnp.where(own[h], vt, jnp.zeros_like(vt)),
                                              preferred_element_type=f32)
        return carry

    lax.fori_loop(0, nck, pv_body, 0)
    outs = []
    for g in range(H_A // 2):
        l_e = jnp.sum(sum_ref[2 * g], axis=-1, keepdims=True)
        l_o = jnp.sum(sum_ref[2 * g + 1], axis=-1, keepdims=True)
        outs.append(acc_ref[g] / jnp.where(own[2 * g], l_e, l_o))
    o_ref[0] = jnp.concatenate(outs, axis=-1)


def _dsa(qkv, qi, kib, kw, *, topk, slopes, ck, tq):
    B, S, _ = qkv.shape
    W = H_A * HEAD_DIM
    WI = IDX_HEADS * IDX_DIM
    return pl.pallas_call(
        functools.partial(_dsa_kernel, topk=topk, slopes=slopes, ck=ck),
        out_shape=jax.ShapeDtypeStruct((B, S, W), f32),
        grid=(B, S // tq),
        in_specs=[pl.BlockSpec((1, tq, W), lambda b, i: (b, i, 0)),
                  pl.BlockSpec((1, S, W), lambda b, i: (b, 0, 1)),
                  pl.BlockSpec((1, S, W), lambda b, i: (b, 0, 2)),
                  pl.BlockSpec((1, tq, WI), lambda b, i: (b, i, 0)),
                  pl.BlockSpec((1, S, LANES), lambda b, i: (b, 0, 0)),
                  pl.BlockSpec((1, tq, LANES), lambda b, i: (b, i, 0))],
        out_specs=pl.BlockSpec((1, tq, W), lambda b, i: (b, i, 0)),
        scratch_shapes=[pltpu.VMEM((S // ck, tq, ck), f32),
                        pltpu.VMEM((S // ck, tq, ck), f32),
                        pltpu.VMEM((H_A, S // ck, tq, ck), f32),
                        pltpu.VMEM((H_A, tq, LANES), f32),
                        pltpu.VMEM((H_A, tq, LANES), f32),
                        pltpu.VMEM((H_A // 2, tq, LANES), f32),
                        pltpu.VMEM((2, tq, LANES), i32),
                        pltpu.VMEM((S // ck, tq, ck), bf16),
                        pltpu.VMEM((tq, LANES), bf16)],
        compiler_params=_params(("parallel", "arbitrary"), 56),
        name="dsa_attention",
    )(qkv, qkv, qkv, qi, kib, kw)


def _sb_kernel(q_ref, k_ref, v_ref, o_ref, *, kb):
    i = pl.program_id(1)
    t0 = i * TQ
    nkb = lax.div(t0 + TQ + kb - 1, kb)
    tpos = t0 + lax.broadcasted_iota(i32, (TQ, 1), 0)
    lane_pos = lax.broadcasted_iota(i32, (1, kb), 1)
    tri = jnp.where(lax.broadcasted_iota(i32, (kb, kb), 0) > lax.broadcasted_iota(i32, (kb, kb), 1),
                    1.0, 0.0).astype(bf16)
    q = q_ref[0]
    qs = [q[:, h * HEAD_DIM:(h + 1) * HEAD_DIM] for h in range(H_B)]

    def cond(carry):
        jj, alive = carry[0], carry[1]
        return (jj < nkb) & (alive > EXP_DEAD)

    def body(carry):
        jj, _, runs, accs = carry
        s0 = pl.multiple_of((nkb - 1 - jj) * kb, kb)
        kt = k_ref[0, pl.ds(s0, kb), :]
        vt = v_ref[0, pl.ds(s0, kb), :]
        strict = (s0 + lane_pos) < tpos
        zs, sps, lms, parts = [], [], [], ([], [], [])
        for h in range(H_B):
            hs = slice(h * HEAD_DIM, (h + 1) * HEAD_DIM)
            z = lax.dot_general(qs[h], kt[:, hs], _NT, preferred_element_type=f32)
            sp = jnp.maximum(z, 0.0) + jnp.log1p(jnp.exp(-jnp.abs(z)))
            lm = jnp.where(strict, -sp, 0.0)
            l1 = lm.astype(bf16)
            r1 = lm - l1.astype(f32)
            l2 = r1.astype(bf16)
            l3 = (r1 - l2.astype(f32)).astype(bf16)
            zs.append(z); sps.append(sp); lms.append(lm)
            parts[0].append(l1); parts[1].append(l2); parts[2].append(l3)
        later = sum(jnp.dot(jnp.concatenate(p, axis=0), tri, preferred_element_type=f32) for p in parts)
        new_runs, new_accs = [], []
        alive = None
        for h in range(H_B):
            hs = slice(h * HEAD_DIM, (h + 1) * HEAD_DIM)
            a = jnp.where(strict, jnp.exp((zs[h] - sps[h]) + (runs[h] + later[h * TQ:(h + 1) * TQ])), 0.0)
            new_accs.append(accs[h] + jnp.dot(a.astype(bf16), vt[:, hs], preferred_element_type=f32))
            run = runs[h] + jnp.sum(lms[h], axis=-1, keepdims=True)
            new_runs.append(run)
            alive = run if alive is None else jnp.maximum(alive, run)
        return jj + 1, jnp.max(alive), tuple(new_runs), tuple(new_accs)

    init = (jnp.int32(0), jnp.float32(0.0),
            tuple(jnp.zeros((TQ, 1), f32) for _ in range(H_B)),
            tuple(jnp.zeros((TQ, HEAD_DIM), f32) for _ in range(H_B)))
    _, _, _, accs = lax.while_loop(cond, body, init)
    o_ref[0] = jnp.concatenate(accs, axis=-1)


def _stick_breaking(qkv, *, kb):
    B, S, _ = qkv.shape
    W = H_B * HEAD_DIM
    return pl.pallas_call(
        functools.partial(_sb_kernel, kb=kb),
        out_shape=jax.ShapeDtypeStruct((B, S, W), f32),
        grid=(B, S // TQ),
        in_specs=[pl.BlockSpec((1, TQ, W), lambda b, i: (b, i, 0)),
                  pl.BlockSpec((1, S, W), lambda b, i: (b, 0, 1)),
                  pl.BlockSpec((1, S, W), lambda b, i: (b, 0, 2))],
        out_specs=pl.BlockSpec((1, TQ, W), lambda b, i: (b, i, 0)),
        compiler_params=_params(("parallel", "arbitrary"), 32),
        name="stick_breaking",
    )(qkv, qkv, qkv)


def _dil_kernel(*refs, dil, slopes, first, last):
    bias_ref, lg_ref, pv_ref = refs[-3:]
    if first:
        q_ref, kp_ref, kc_ref, vp_ref, vc_ref, acc_out, st_out = refs[:-3]
    elif last:
        q_ref, kp_ref, kc_ref, vp_ref, vc_ref, acc_in, st_in, o_ref = refs[:-3]
    else:
        q_ref, kp_ref, kc_ref, vp_ref, vc_ref, acc_in, st_in, acc_out, st_out = refs[:-3]
    j = pl.program_id(1)
    r = pl.program_id(2)
    rows = pl.ds(r, TQ, stride=dil) if dil > 1 else slice(None)

    @pl.when((pl.program_id(0) == 0) & (j == 0) & (r == 0))
    def _():
        u = lax.broadcasted_iota(i32, (TQ, 1), 0)
        c = lax.broadcasted_iota(i32, (1, TQ), 1)
        d_cur = u - c
        d_prev = d_cur + TQ
        for h in range(H_C):
            bias_ref[h, 0] = jnp.where(d_prev <= TQ, d_prev.astype(f32) * (-slopes[h] * dil), NEG)
            bias_ref[h, 1] = jnp.where(d_cur >= 0, d_cur.astype(f32) * (-slopes[h] * dil), NEG)

    pen = jnp.where(j > 0, 0.0, NEG)
    lane = lax.broadcasted_iota(i32, (TQ, LANES), 1)
    lo_half = lane < HEAD_DIM
    owns = (lo_half, jnp.logical_not(lo_half))

    if not first:
        st = st_in[0, rows, :]

    ms = []
    for h in range(H_C):
        ts = slice((h // 2) * LANES, (h // 2 + 1) * LANES)
        qt = q_ref[0, 0, :, ts]
        qm = jnp.where(owns[h % 2], qt, jnp.zeros_like(qt))
        lg_p = lax.dot_general(qm, kp_ref[0, 0, :, ts], _NT, preferred_element_type=f32) + (bias_ref[h, 0] + pen)
        lg_c = lax.dot_general(qm, kc_ref[0, 0, :, ts], _NT, preferred_element_type=f32) + bias_ref[h, 1]
        lg_ref[h, 0] = lg_p
        lg_ref[h, 1] = lg_c
        m = jnp.max(jnp.maximum(lg_p, lg_c), axis=-1, keepdims=True)
        if not first:
            m = jnp.maximum(m, st[:, h:h + 1])
        ms.append(m)

    for h in range(H_C):
        ts = slice((h // 2) * LANES, (h // 2 + 1) * LANES)
        vpt, vct = vp_ref[0, 0, :, ts], vc_ref[0, 0, :, ts]
        one = jnp.ones_like(vpt)
        p_p = jnp.exp(lg_ref[h, 0] - ms[h]).astype(bf16)
        p_c = jnp.exp(lg_ref[h, 1] - ms[h]).astype(bf16)
        pv_ref[h] = (jnp.dot(p_p, jnp.where(owns[h % 2], vpt, one), preferred_element_type=f32)
                     + jnp.dot(p_c, jnp.where(owns[h % 2], vct, one), preferred_element_type=f32))

    st_new = jnp.zeros((TQ, LANES), f32)
    for g in range(H_C // 2):
        ts = slice(g * LANES, (g + 1) * LANES)
        pv0, pv1 = pv_ref[2 * g], pv_ref[2 * g + 1]
        acc_t = jnp.where(lo_half, pv0, pv1)
        l_t = pltpu.roll(jnp.where(lo_half, pv1, pv0), HEAD_DIM, 1)
        if not first:
            alpha = jnp.exp(jnp.where(lo_half, st[:, 2 * g:2 * g + 1] - ms[2 * g],
                                      st[:, 2 * g + 1:2 * g + 2] - ms[2 * g + 1]))
            l_old = jnp.where(lo_half, st[:, H_C + 2 * g:H_C + 2 * g + 1], st[:, H_C + 2 * g + 1:H_C + 2 * g + 2])
            acc_t = acc_t + alpha * acc_in[0, g, rows, :]
            l_t = l_t + alpha * l_old
        if last:
            o_ref[0, :, ts] = acc_t / l_t
        else:
            acc_out[0, g, rows, :] = acc_t
            for e in range(2):
                h = 2 * g + e
                st_new = jnp.where(lane == h, ms[h], st_new)
                st_new = jnp.where(lane == H_C + h, l_t[:, e * HEAD_DIM:e * HEAD_DIM + 1], st_new)
    if not last:
        st_out[0, rows, :] = st_new


def _dilated_branch(qkv_cls, acc, st, *, slopes, first, last):
    B, dil, Lc, W3 = qkv_cls.shape
    W = W3 // 3
    S = dil * Lc
    nj = Lc // TQ
    n_slabs = W // LANES
    assert last == (dil == 1)
    blk = lambda comp, prev: pl.BlockSpec(
        (1, 1, TQ, W), (lambda b, j, r: (b, r, jnp.maximum(j - 1, 0), comp)) if prev
        else (lambda b, j, r: (b, r, j, comp)))
    acc_spec = pl.BlockSpec((1, n_slabs, dil * TQ, LANES), lambda b, j, r: (b, 0, j, 0))
    st_spec = pl.BlockSpec((1, dil * TQ, LANES), lambda b, j, r: (b, j, 0))
    in_specs = [blk(0, False), blk(1, True), blk(1, False), blk(2, True), blk(2, False)]
    args = [qkv_cls] * 5
    if not first:
        in_specs += [acc_spec, st_spec]
        args += [acc, st]
    if last:
        out_shape = jax.ShapeDtypeStruct((B, S, W), f32)
        out_specs = pl.BlockSpec((1, TQ, W), lambda b, j, r: (b, j, 0))
    else:
        out_shape = [jax.ShapeDtypeStruct((B, n_slabs, S, LANES), f32), jax.ShapeDtypeStruct((B, S, LANES), f32)]
        out_specs = [acc_spec, st_spec]
    return pl.pallas_call(
        functools.partial(_dil_kernel, dil=dil, slopes=slopes, first=first, last=last),
        out_shape=out_shape,
        grid=(B, nj, dil),
        in_specs=in_specs,
        out_specs=out_specs,
        scratch_shapes=[pltpu.VMEM((H_C, 2, TQ, TQ), f32),
                        pltpu.VMEM((H_C, 2, TQ, TQ), f32),
                        pltpu.VMEM((H_C, TQ, LANES), f32)],
        compiler_params=_params(("arbitrary", "arbitrary", "arbitrary"), 40),
        name=f"dilated_d{dil}",
    )(*args)


def _dilated(qkv_by_dil, slopes):
    dils = sorted(qkv_by_dil, reverse=True)
    acc = st = None
    for n, dil in enumerate(dils):
        first, last = n == 0, n == len(dils) - 1
        res = _dilated_branch(qkv_by_dil[dil], acc, st, slopes=slopes, first=first, last=last)
        if last:
            return res
        acc, st = res


def _out_kernel(oa_ref, ob_ref, oc_ref, x_ref, gt_ref, go_ref, w_ref, o_ref):
    go = go_ref[...]
    y = None
    o = 0
    for ref in (oa_ref, ob_ref, oc_ref):
        v = ref[0]
        w = v.shape[1]
        ms = jnp.mean(v * v, axis=-1, keepdims=True)
        yn = (v * lax.rsqrt(ms + EPS) * go[:, o:o + w]).astype(bf16)
        part = jnp.dot(yn, w_ref[o:o + w, :], preferred_element_type=f32)
        y = part if y is None else y + part
        o += w
    o_ref[0] = x_ref[0] + gt_ref[0] * y


def _out_proj(oa, ob, oc, x, gt, go, w, *, tm):
    B, S, D = x.shape
    blk = lambda a: pl.BlockSpec((1, tm, a.shape[2]), lambda b, j: (b, j, 0))
    return pl.pallas_call(
        _out_kernel,
        out_shape=jax.ShapeDtypeStruct((B, S, D), f32),
        grid=(B, S // tm),
        in_specs=[blk(oa), blk(ob), blk(oc), blk(x), pl.BlockSpec((1, 1, D), lambda b, j: (b, 0, 0)),
                  _resident(go.shape), _resident(w.shape)],
        out_specs=blk(x),
        compiler_params=_params(("parallel", "parallel"), 40),
        name="mixer_out_proj",
    )(oa, ob, oc, x, gt, go, w)


def _regroup_w_in(w_in):
    WA, WB, WC = H_A * HEAD_DIM, H_B * HEAD_DIM, H_C * HEAD_DIM
    WI = IDX_HEADS * IDX_DIM
    sizes = [WA] * 3 + [WI, IDX_DIM, IDX_HEADS] + [WB] * 3 + [WC] * 3
    offs = np.concatenate([[0], np.cumsum(sizes)])
    seg = [w_in[:, offs[n]:offs[n + 1]] for n in range(len(sizes))]
    qa, ka, va, qi, ki, wi, qb, kb, vb, qc, kc, vc = seg
    pad = jnp.zeros((w_in.shape[0], LANES - IDX_DIM - IDX_HEADS), w_in.dtype)
    return jnp.concatenate([qa, ka, va, qb, kb, vb, qc, kc, vc, qi, ki, wi, pad], axis=1).astype(bf16)


def kernel(x, c, w_ada, b_ada, norm_ffn1, w_gate1, w_up1, w_down1, norm_mix, w_in, qn_a, kn_a, qn_c, kn_c,
           g_out, w_out, norm_ffn2, w_gate2, w_up2, w_down2):
    B, S, D = x.shape
    L = w_ada.shape[0]
    topk = min(TOPK_MAX, S // 4)
    slopes_a, slopes_c = _alibi_slopes()
    tm = min(512, S)
    assert all(window // dil == TQ for window, dil in DILATIONS)
    wide_dils = tuple(sorted(dil for _, dil in DILATIONS if dil > 1))
    mod = _modulation(c, w_ada, b_ada).reshape(L, B, N_MOD, 1, D)
    tile = lambda g, n: jnp.tile(g, n).reshape(1, -1)
    for l in range(L):
        sh1, sc1, g1, sh2, sc2, g2, sh3, sc3, g3 = [mod[l, :, n] for n in range(N_MOD)]
        x = _ffn(x, sh1, sc1, g1, norm_ffn1[l].reshape(1, D), w_gate1[l].astype(bf16), w_up1[l].astype(bf16),
                 w_down1[l].astype(bf16), tm=tm, n_chunks=2)
        qkv_a, qkv_b, qkv_c, qi, kib, kw, *qkv_cls = _proj(
            x, sh2, sc2, norm_mix[l].reshape(1, D), _regroup_w_in(w_in[l]),
            tile(qn_a[l], H_A), tile(kn_a[l], H_A), tile(qn_c[l], H_C), tile(kn_c[l], H_C), tm=tm, dils=wide_dils)
        o_a = _dsa(qkv_a, qi, kib, kw, topk=topk, slopes=slopes_a, ck=min(512, S), tq=min(256, S))
        o_b = _stick_breaking(qkv_b, kb=256)
        o_c = _dilated({1: qkv_c.reshape(B, 1, S, qkv_c.shape[2]), **dict(zip(wide_dils, qkv_cls))}, slopes_c)
        x = _out_proj(o_a, o_b, o_c, x, g2, g_out[l].reshape(1, -1), w_out[l].astype(bf16), tm=tm)
        x = _ffn(x, sh3, sc3, g3, norm_ffn2[l].reshape(1, D), w_gate2[l].astype(bf16), w_up2[l].astype(bf16),
                 w_down2[l].astype(bf16), tm=tm, n_chunks=2)
    return x
```

```python
import functools

import numpy as np
import jax
import jax.numpy as jnp
from jax import lax
from jax.experimental import pallas as pl
from jax.experimental.pallas import tpu as pltpu

HEAD_DIM = 64
H_A = 4
H_B = 4
H_C = 8
IDX_HEADS = 8
IDX_DIM = 64
TOPK_MAX = 256
DILATIONS = ((128, 1), (512, 4), (2048, 16))
N_MOD = 9
EPS = 1e-6

TQ = 128
LANES = 128
ROW_GROUP = 128
NEG = -1e30
INT_MIN = -2 ** 31
NEG_INF_KEY = -2139095041
EXP_DEAD = -104.0
V7X_VMEM_BYTES = 64 * 1024 * 1024

f32 = jnp.float32
bf16 = jnp.bfloat16
i32 = jnp.int32

_NT = (((1,), (1,)), ((), ()))


def _params(sem, vmem_mb):
    return pltpu.CompilerParams(dimension_semantics=sem,
                                vmem_limit_bytes=min(vmem_mb * 1024 * 1024, V7X_VMEM_BYTES - (4 << 20)))


def _alibi_slopes():
    n = H_A + H_C
    s = np.exp2(-8.0 * np.arange(1, n + 1, dtype=np.float32) / np.float32(n)).astype(np.float32)
    idx = np.arange(n)
    return [float(v) for v in s[idx % 3 == 2]], [float(v) for v in s[idx % 3 != 2]]


def _resident(shape):
    nd = len(shape)
    return pl.BlockSpec(shape, lambda *_: (0,) * nd, pipeline_mode=pl.Buffered(1))


def _mod_kernel(c_ref, w_ref, b_ref, o_ref):
    c = c_ref[...]
    cond = c * jax.nn.sigmoid(c)
    o_ref[0] = jnp.dot(cond.astype(bf16), w_ref[0].astype(bf16), preferred_element_type=f32) + b_ref[0]


def _modulation(c, w_ada, b_ada):
    L, D, ND = w_ada.shape
    B = c.shape[0]
    tn = D
    return pl.pallas_call(
        _mod_kernel,
        out_shape=jax.ShapeDtypeStruct((L, B, ND), f32),
        grid=(L, ND // tn),
        in_specs=[pl.BlockSpec((B, D), lambda l, n: (0, 0)),
                  pl.BlockSpec((1, D, tn), lambda l, n: (l, 0, n)),
                  pl.BlockSpec((1, 1, tn), lambda l, n: (l, 0, n))],
        out_specs=pl.BlockSpec((1, B, tn), lambda l, n: (l, 0, n)),
        compiler_params=_params(("arbitrary", "arbitrary"), 32),
        name="adaln_mod",
    )(c, w_ada, b_ada.reshape(L, 1, ND))


def _norm_mod(x, nw, sc, sh):
    ms = jnp.mean(x * x, axis=-1, keepdims=True)
    y = x * lax.rsqrt(ms + EPS) * nw
    return y * (1.0 + sc) + sh


def _ffn_kernel(x_ref, sh_ref, sc_ref, gt_ref, nw_ref, wg_ref, wu_ref, wd_ref, o_ref, *, n_chunks):
    x = x_ref[0]
    h = _norm_mod(x, nw_ref[...], sc_ref[0], sh_ref[0]).astype(bf16)
    F = wg_ref.shape[1]
    fc = F // n_chunks
    y = None
    for c in range(n_chunks):
        g = jnp.dot(h, wg_ref[:, c * fc:(c + 1) * fc], preferred_element_type=f32)
        u = jnp.dot(h, wu_ref[:, c * fc:(c + 1) * fc], preferred_element_type=f32)
        a = (g * jax.nn.sigmoid(g) * u).astype(bf16)
        part = jnp.dot(a, wd_ref[c * fc:(c + 1) * fc, :], preferred_element_type=f32)
        y = part if y is None else y + part
    o_ref[0] = x + (0.5 * gt_ref[0]) * y


def _ffn(x, sh, sc, gt, nw, wg, wu, wd, *, tm, n_chunks):
    B, S, D = x.shape
    F = wg.shape[1]
    vec = pl.BlockSpec((1, 1, D), lambda b, j: (b, 0, 0))
    return pl.pallas_call(
        functools.partial(_ffn_kernel, n_chunks=n_chunks),
        out_shape=jax.ShapeDtypeStruct((B, S, D), f32),
        grid=(B, S // tm),
        in_specs=[pl.BlockSpec((1, tm, D), lambda b, j: (b, j, 0)), vec, vec, vec,
                  _resident((1, D)), _resident((D, F)), _resident((D, F)), _resident((F, D))],
        out_specs=pl.BlockSpec((1, tm, D), lambda b, j: (b, j, 0)),
        compiler_params=_params(("parallel", "parallel"), 56),
        name="ffn_swiglu",
    )(x, sh, sc, gt, nw, wg, wu, wd)


def _head_rms(x, gain_row, scale):
    W = x.shape[1]
    outs = []
    r = lax.broadcasted_iota(i32, (256, 256), 0) >> 6
    c = lax.broadcasted_iota(i32, (256, 256), 1) >> 6
    bd = jnp.where(r == c, 1.0, 0.0).astype(bf16)
    for s in range(W // 256):
        xs = x[:, s * 256:(s + 1) * 256]
        sq = xs * xs
        hi = sq.astype(bf16)
        lo = (sq - hi.astype(f32)).astype(bf16)
        ss = jnp.dot(hi, bd, preferred_element_type=f32) + jnp.dot(lo, bd, preferred_element_type=f32)
        y = xs * lax.rsqrt(ss * (1.0 / HEAD_DIM) + EPS) * gain_row[:, s * 256:(s + 1) * 256]
        outs.append(y * scale if scale != 1.0 else y)
    return outs[0] if len(outs) == 1 else jnp.concatenate(outs, axis=-1)


def _proj_kernel(x_ref, sh_ref, sc_ref, nw_ref, w_ref, qna_ref, kna_ref, qnc_ref, knc_ref,
                 a_ref, b_ref, c_ref, qi_ref, kib_ref, kw_ref, *rest, dils):
    class_refs, slab_ref = rest[:-1], rest[-1]
    tm = x_ref.shape[1]
    x = x_ref[0]
    h = _norm_mod(x, nw_ref[...], sc_ref[0], sh_ref[0]).astype(bf16)
    WA, WB, WC = H_A * HEAD_DIM, H_B * HEAD_DIM, H_C * HEAD_DIM
    qscale = HEAD_DIM ** -0.5
    o = 0

    def mm(width):
        nonlocal o
        r = jnp.dot(h, w_ref[:, o:o + width], preferred_element_type=f32)
        o += width
        return r

    pa = mm(3 * WA)
    a_ref[0] = jnp.concatenate([_head_rms(pa[:, :WA], qna_ref[...], qscale),
                                _head_rms(pa[:, WA:2 * WA], kna_ref[...], 1.0),
                                pa[:, 2 * WA:]], axis=-1).astype(bf16)
    pb = mm(3 * WB)
    b_ref[0] = jnp.concatenate([pb[:, :WB] * qscale, pb[:, WB:]], axis=-1).astype(bf16)
    pc = mm(3 * WC)
    qkv_c = jnp.concatenate([_head_rms(pc[:, :WC], qnc_ref[...], qscale),
                             _head_rms(pc[:, WC:2 * WC], knc_ref[...], 1.0),
                             pc[:, 2 * WC:]], axis=-1)
    c_ref[0] = qkv_c.astype(bf16)
    n_slabs = 3 * WC // LANES
    for g in range(n_slabs):
        slab_ref[g] = qkv_c[:, g * LANES:(g + 1) * LANES]
    for ref, dil in zip(class_refs, dils):
        for r in range(dil):
            for g in range(n_slabs):
                ref[0, r, :, g * LANES:(g + 1) * LANES] = slab_ref[g, pl.ds(r, tm // dil, stride=dil), :].astype(bf16)
    qi_ref[0] = mm(IDX_HEADS * IDX_DIM).astype(bf16)
    kw = mm(LANES)
    kib_ref[0] = kw.astype(bf16)
    kw_ref[0] = kw


def _proj(x, sh, sc, nw, w2, qna, kna, qnc, knc, *, tm, dils):
    B, S, D = x.shape
    WA, WB, WC = H_A * HEAD_DIM, H_B * HEAD_DIM, H_C * HEAD_DIM
    WI = IDX_HEADS * IDX_DIM
    vec = pl.BlockSpec((1, 1, D), lambda b, j: (b, 0, 0))
    widths = (3 * WA, 3 * WB, 3 * WC, WI, LANES, LANES)
    dtypes = (bf16, bf16, bf16, bf16, bf16, f32)
    out_shape = [jax.ShapeDtypeStruct((B, S, w), dt) for w, dt in zip(widths, dtypes)]
    out_specs = [pl.BlockSpec((1, tm, w), lambda b, j: (b, j, 0)) for w in widths]
    for dil in dils:
        out_shape.append(jax.ShapeDtypeStruct((B, dil, S // dil, 3 * WC), bf16))
        out_specs.append(pl.BlockSpec((1, dil, tm // dil, 3 * WC), lambda b, j: (b, 0, j, 0)))
    return pl.pallas_call(
        functools.partial(_proj_kernel, dils=dils),
        out_shape=out_shape,
        grid=(B, S // tm),
        in_specs=[pl.BlockSpec((1, tm, D), lambda b, j: (b, j, 0)), vec, vec,
                  _resident((1, D)), _resident(w2.shape),
                  _resident((1, WA)), _resident((1, WA)), _resident((1, WC)), _resident((1, WC))],
        out_specs=out_specs,
        scratch_shapes=[pltpu.VMEM((3 * WC // LANES, tm, LANES), f32)],
        compiler_params=_params(("parallel", "parallel"), 56),
        name="mixer_in_proj",
    )(x, sh, sc, nw, w2, qna, kna, qnc, knc)


def _dsa_kernel(q_ref, k_ref, v_ref, qi_ref, ki_ref, w_ref, o_ref,
                sc_ref, bias_ref, logit_ref, mx_ref, sum_ref, acc_ref, opnd_ref, *, topk, slopes, ck):
    TQ = q_ref.shape[1]
    S = k_ref.shape[1]
    i = pl.program_id(1)
    t0 = i * TQ
    nck = lax.div(t0 + TQ + ck - 1, ck)
    lt = ck // LANES
    tpos = t0 + lax.broadcasted_iota(i32, (TQ, 1), 0)
    lane_pos = lax.broadcasted_iota(i32, (1, ck), 1)
    idx_scale = (IDX_DIM ** -0.5) * (IDX_HEADS ** -0.5)

    def fold(m):
        r = m[:, 0:LANES]
        for t in range(1, lt):
            r = r + m[:, t * LANES:(t + 1) * LANES]
        return r

    def fold_max(x):
        r = x[:, 0:LANES]
        for t in range(1, lt):
            r = jnp.maximum(r, x[:, t * LANES:(t + 1) * LANES])
        return r

    def fold_min(x):
        r = x[:, 0:LANES]
        for t in range(1, lt):
            r = jnp.minimum(r, x[:, t * LANES:(t + 1) * LANES])
        return r

    def to_key(x):
        bits = lax.bitcast_convert_type(x, i32)
        return bits ^ ((bits >> 31) & 0x7FFFFFFF)

    def from_key(k):
        return lax.bitcast_convert_type(k ^ ((k >> 31) & 0x7FFFFFFF), f32)

    def count(pred_fn, *cols):
        lane_t = lax.broadcasted_iota(i32, (1, LANES), 1)
        groups = [slice(r0, r0 + ROW_GROUP) for r0 in range(0, TQ, ROW_GROUP)]
        for n, col in enumerate(cols):
            opnd_ref[n] = lax.bitcast_convert_type(jnp.broadcast_to(col, (TQ, LANES)), i32)

        def body(c, accs):
            new = []
            for rs, acc in zip(groups, accs):
                tiles = [lax.bitcast_convert_type(opnd_ref[n, rs, :], col.dtype) for n, col in enumerate(cols)]
                sc = sc_ref[c, rs, :]
                for t in range(lt):
                    m = pred_fn(sc[:, t * LANES:(t + 1) * LANES], c * ck + t * LANES + lane_t, *tiles)
                    acc = acc + jnp.where(m, 1, 0).astype(i32)
                new.append(acc)
            return tuple(new)
        accs = lax.fori_loop(0, nck, body, tuple(jnp.zeros((ROW_GROUP, LANES), i32) for _ in groups))
        acc = accs[0] if len(accs) == 1 else jnp.concatenate(accs, axis=0)
        return jnp.sum(acc.astype(f32), axis=-1, keepdims=True)

    qi = qi_ref[0]
    qstack = jnp.concatenate([qi[:, h * IDX_DIM:(h + 1) * IDX_DIM] for h in range(IDX_HEADS)], axis=0)
    wts = w_ref[0]
    wcols = [wts[:, IDX_DIM + h:IDX_DIM + h + 1] for h in range(IDX_HEADS)]

    def score_body(c, carry):
        c0 = pl.multiple_of(c * ck, ck)
        kic = ki_ref[0, pl.ds(c0, ck), :][:, :IDX_DIM]
        rel = lax.dot_general(qstack, kic, _NT, preferred_element_type=f32)
        score = wcols[0] * jnp.maximum(rel[0:TQ], 0.0)
        for h in range(1, IDX_HEADS):
            score = score + wcols[h] * jnp.maximum(rel[h * TQ:(h + 1) * TQ], 0.0)
        score = score * idx_scale
        score = jnp.where(score == 0.0, 0.0, score)
        sc_ref[c] = jnp.where(c0 + lane_pos <= tpos, score, -jnp.inf)
        return carry

    lax.fori_loop(0, nck, score_body, 0)

    kf = float(topk)

    def bit_body(b, p):
        cand_u = p | lax.shift_left(jnp.int32(1), 31 - b)
        cnt = count(lambda sc, kp, cand: sc >= cand, from_key(cand_u ^ INT_MIN))
        return jnp.where(cnt >= kf, cand_u, p)

    tau_key = lax.fori_loop(0, 32, bit_body, jnp.zeros((TQ, 1), i32)) ^ INT_MIN
    tau = from_key(tau_key)
    tau_eff = from_key(jnp.maximum(tau_key, NEG_INF_KEY + 1))

    cnt_gt = count(lambda sc, kp, t: sc > t, tau)
    cnt_ge = count(lambda sc, kp, t: sc >= t, tau)
    need = (cnt_ge > kf) & (tau_key > NEG_INF_KEY)
    any_need = jnp.max(jnp.where(need, 1.0, 0.0)) > 0.0

    @pl.when(jnp.logical_not(any_need))
    def _():
        def bias_body(c, carry):
            bias_ref[c] = jnp.where(sc_ref[c] >= tau_eff, 0.0, NEG)
            return carry

        lax.fori_loop(0, nck, bias_body, 0)

    @pl.when(any_need)
    def _():
        quota = kf - cnt_gt
        nbits = max(1, (S - 1).bit_length())

        def jbit(b, lo):
            cand = lo | lax.shift_left(jnp.int32(1), nbits - 1 - b)
            c = count(lambda sc, kp, t, cd: (sc == t) & (kp < cd), tau, cand)
            return jnp.where(c < quota, cand, lo)

        jstar = lax.fori_loop(0, nbits, jbit, jnp.zeros((TQ, 1), i32))

        def bias_body(c, carry):
            sc = sc_ref[c]
            kp = c * ck + lane_pos
            tied = (sc > tau) | ((sc == tau) & (kp <= jstar))
            plain = jnp.logical_not(need) & (sc >= tau_eff)
            sel = ((need & tied) | plain) & (kp <= tpos)
            bias_ref[c] = jnp.where(sel, 0.0, NEG)
            return carry

        lax.fori_loop(0, nck, bias_body, 0)

    lane = lax.broadcasted_iota(i32, (1, LANES), 1)
    own = [(lane < HEAD_DIM) if h % 2 == 0 else (lane >= HEAD_DIM) for h in range(H_A)]
    qm = []
    for h in range(H_A):
        qt = q_ref[0, :, (h // 2) * LANES:(h // 2 + 1) * LANES]
        qm.append(jnp.where(own[h], qt, jnp.zeros_like(qt)))

    mx_ref[...] = jnp.full(mx_ref.shape, NEG, f32)

    def logit_body(c, carry):
        c0 = pl.multiple_of(c * ck, ck)
        bias = bias_ref[c]
        pos = (c0 + lane_pos).astype(f32)
        for h in range(H_A):
            kt = k_ref[0, pl.ds(c0, ck), (h // 2) * LANES:(h // 2 + 1) * LANES]
            lg = lax.dot_general(qm[h], kt, _NT, preferred_element_type=f32) + (slopes[h] * pos + bias)
            logit_ref[h, c] = lg
            mx_ref[h] = jnp.maximum(mx_ref[h], fold_max(lg))
        return carry

    lax.fori_loop(0, nck, logit_body, 0)
    row_max = [jnp.max(mx_ref[h], axis=-1, keepdims=True) for h in range(H_A)]
    sum_ref[...] = jnp.zeros(sum_ref.shape, f32)
    acc_ref[...] = jnp.zeros(acc_ref.shape, f32)

    def pv_body(c, carry):
        c0 = pl.multiple_of(c * ck, ck)
        for h in range(H_A):
            g = h // 2
            vt = v_ref[0, pl.ds(c0, ck), g * LANES:(g + 1) * LANES]
            p = jnp.exp(logit_ref[h, c] - row_max[h])
            sum_ref[h] = sum_ref[h] + fold(p)
            acc_ref[g] = acc_ref[g] + jnp.dot(p.astype(bf16), jnp.where(own[h], vt, jnp.zeros_like(vt)),
                                              preferred_element_type=f32)
        return carry

    lax.fori_loop(0, nck, pv_body, 0)
    outs = []
    for g in range(H_A // 2):
        l_e = jnp.sum(sum_ref[2 * g], axis=-1, keepdims=True)
        l_o = jnp.sum(sum_ref[2 * g + 1], axis=-1, keepdims=True)
        outs.append(acc_ref[g] / jnp.where(own[2 * g], l_e, l_o))
    o_ref[0] = jnp.concatenate(outs, axis=-1)


def _dsa(qkv, qi, kib, kw, *, topk, slopes, ck, tq):
    B, S, _ = qkv.shape
    W = H_A * HEAD_DIM
    WI = IDX_HEADS * IDX_DIM
    return pl.pallas_call(
        functools.partial(_dsa_kernel, topk=topk, slopes=slopes, ck=ck),
        out_shape=jax.ShapeDtypeStruct((B, S, W), f32),
        grid=(B, S // tq),
        in_specs=[pl.BlockSpec((1, tq, W), lambda b, i: (b, i, 0)),
                  pl.BlockSpec((1, S, W), lambda b, i: (b, 0, 1)),
                  pl.BlockSpec((1, S, W), lambda b, i: (b, 0, 2)),
                  pl.BlockSpec((1, tq, WI), lambda b, i: (b, i, 0)),
                  pl.BlockSpec((1, S, LANES), lambda b, i: (b, 0, 0)),
                  pl.BlockSpec((1, tq, LANES), lambda b, i: (b, i, 0))],
        out_specs=pl.BlockSpec((1, tq, W), lambda b, i: (b, i, 0)),
        scratch_shapes=[pltpu.VMEM((S // ck, tq, ck), f32),
                        pltpu.VMEM((S // ck, tq, ck), f32),
                        pltpu.VMEM((H_A, S // ck, tq, ck), f32),
                        pltpu.VMEM((H_A, tq, LANES), f32),
                        pltpu.VMEM((H_A, tq, LANES), f32),
                        pltpu.VMEM((H_A // 2, tq, LANES), f32),
                        pltpu.VMEM((2, tq, LANES), i32)],
        compiler_params=_params(("parallel", "arbitrary"), 56),
        name="dsa_attention",
    )(qkv, qkv, qkv, qi, kib, kw)


def _sb_kernel(q_ref, k_ref, v_ref, o_ref, *, kb):
    TQ = q_ref.shape[1]
    i = pl.program_id(1)
    t0 = i * TQ
    nkb = lax.div(t0 + TQ + kb - 1, kb)
    tpos = t0 + lax.broadcasted_iota(i32, (TQ, 1), 0)
    lane_pos = lax.broadcasted_iota(i32, (1, kb), 1)
    tri = jnp.where(lax.broadcasted_iota(i32, (kb, kb), 0) > lax.broadcasted_iota(i32, (kb, kb), 1),
                    1.0, 0.0).astype(bf16)
    q = q_ref[0]
    qs = [q[:, h * HEAD_DIM:(h + 1) * HEAD_DIM] for h in range(H_B)]

    def cond(carry):
        jj, alive = carry[0], carry[1]
        return (jj < nkb) & (alive > EXP_DEAD)

    def body(carry):
        jj, _, runs, accs = carry
        s0 = pl.multiple_of((nkb - 1 - jj) * kb, kb)
        kt = k_ref[0, pl.ds(s0, kb), :]
        vt = v_ref[0, pl.ds(s0, kb), :]
        strict = (s0 + lane_pos) < tpos
        zs, sps, lms, parts = [], [], [], ([], [], [])
        for h in range(H_B):
            hs = slice(h * HEAD_DIM, (h + 1) * HEAD_DIM)
            z = lax.dot_general(qs[h], kt[:, hs], _NT, preferred_element_type=f32)
            sp = jnp.maximum(z, 0.0) + jnp.log1p(jnp.exp(-jnp.abs(z)))
            lm = jnp.where(strict, -sp, 0.0)
            l1 = lm.astype(bf16)
            r1 = lm - l1.astype(f32)
            l2 = r1.astype(bf16)
            l3 = (r1 - l2.astype(f32)).astype(bf16)
            zs.append(z); sps.append(sp); lms.append(lm)
            parts[0].append(l1); parts[1].append(l2); parts[2].append(l3)
        later = sum(jnp.dot(jnp.concatenate(p, axis=0), tri, preferred_element_type=f32) for p in parts)
        new_runs, new_accs = [], []
        alive = None
        for h in range(H_B):
            hs = slice(h * HEAD_DIM, (h + 1) * HEAD_DIM)
            a = jnp.where(strict, jnp.exp((zs[h] - sps[h]) + (runs[h] + later[h * TQ:(h + 1) * TQ])), 0.0)
            new_accs.append(accs[h] + jnp.dot(a.astype(bf16), vt[:, hs], preferred_element_type=f32))
            run = runs[h] + jnp.sum(lms[h], axis=-1, keepdims=True)
            new_runs.append(run)
            alive = run if alive is None else jnp.maximum(alive, run)
        return jj + 1, jnp.max(alive), tuple(new_runs), tuple(new_accs)

    init = (jnp.int32(0), jnp.float32(0.0),
            tuple(jnp.zeros((TQ, 1), f32) for _ in range(H_B)),
            tuple(jnp.zeros((TQ, HEAD_DIM), f32) for _ in range(H_B)))
    _, _, _, accs = lax.while_loop(cond, body, init)
    o_ref[0] = jnp.concatenate(accs, axis=-1)


def _stick_breaking(qkv, *, kb, tq):
    B, S, _ = qkv.shape
    W = H_B * HEAD_DIM
    return pl.pallas_call(
        functools.partial(_sb_kernel, kb=kb),
        out_shape=jax.ShapeDtypeStruct((B, S, W), f32),
        grid=(B, S // tq),
        in_specs=[pl.BlockSpec((1, tq, W), lambda b, i: (b, i, 0)),
                  pl.BlockSpec((1, S, W), lambda b, i: (b, 0, 1)),
                  pl.BlockSpec((1, S, W), lambda b, i: (b, 0, 2))],
        out_specs=pl.BlockSpec((1, tq, W), lambda b, i: (b, i, 0)),
        compiler_params=_params(("parallel", "arbitrary"), 32),
        name="stick_breaking",
    )(qkv, qkv, qkv)


def _dil_kernel(*refs, dil, slopes, first, last):
    bias_ref, lg_ref, pv_ref = refs[-3:]
    if first:
        q_ref, kp_ref, kc_ref, vp_ref, vc_ref, acc_out, st_out = refs[:-3]
    elif last:
        q_ref, kp_ref, kc_ref, vp_ref, vc_ref, acc_in, st_in, o_ref = refs[:-3]
    else:
        q_ref, kp_ref, kc_ref, vp_ref, vc_ref, acc_in, st_in, acc_out, st_out = refs[:-3]
    j = pl.program_id(1)
    r = pl.program_id(2)
    rows = pl.ds(r, TQ, stride=dil) if dil > 1 else slice(None)

    @pl.when((pl.program_id(0) == 0) & (j == 0) & (r == 0))
    def _():
        u = lax.broadcasted_iota(i32, (TQ, 1), 0)
        c = lax.broadcasted_iota(i32, (1, TQ), 1)
        d_cur = u - c
        d_prev = d_cur + TQ
        for h in range(H_C):
            bias_ref[h, 0] = jnp.where(d_prev <= TQ, d_prev.astype(f32) * (-slopes[h] * dil), NEG)
            bias_ref[h, 1] = jnp.where(d_cur >= 0, d_cur.astype(f32) * (-slopes[h] * dil), NEG)

    pen = jnp.where(j > 0, 0.0, NEG)
    lane = lax.broadcasted_iota(i32, (TQ, LANES), 1)
    lo_half = lane < HEAD_DIM
    owns = (lo_half, jnp.logical_not(lo_half))

    if not first:
        st = st_in[0, rows, :]

    ms = []
    for h in range(H_C):
        ts = slice((h // 2) * LANES, (h // 2 + 1) * LANES)
        qt = q_ref[0, 0, :, ts]
        qm = jnp.where(owns[h % 2], qt, jnp.zeros_like(qt))
        lg_p = lax.dot_general(qm, kp_ref[0, 0, :, ts], _NT, preferred_element_type=f32) + (bias_ref[h, 0] + pen)
        lg_c = lax.dot_general(qm, kc_ref[0, 0, :, ts], _NT, preferred_element_type=f32) + bias_ref[h, 1]
        lg_ref[h, 0] = lg_p
        lg_ref[h, 1] = lg_c
        m = jnp.max(jnp.maximum(lg_p, lg_c), axis=-1, keepdims=True)
        if not first:
            m = jnp.maximum(m, st[:, h:h + 1])
        ms.append(m)

    for h in range(H_C):
        ts = slice((h // 2) * LANES, (h // 2 + 1) * LANES)
        vpt, vct = vp_ref[0, 0, :, ts], vc_ref[0, 0, :, ts]
        one = jnp.ones_like(vpt)
        p_p = jnp.exp(lg_ref[h, 0] - ms[h]).astype(bf16)
        p_c = jnp.exp(lg_ref[h, 1] - ms[h]).astype(bf16)
        pv_ref[h] = (jnp.dot(p_p, jnp.where(owns[h % 2], vpt, one), preferred_element_type=f32)
                     + jnp.dot(p_c, jnp.where(owns[h % 2], vct, one), preferred_element_type=f32))

    st_new = jnp.zeros((TQ, LANES), f32)
    for g in range(H_C // 2):
        ts = slice(g * LANES, (g + 1) * LANES)
        pv0, pv1 = pv_ref[2 * g], pv_ref[2 * g + 1]
        acc_t = jnp.where(lo_half, pv0, pv1)
        l_t = pltpu.roll(jnp.where(lo_half, pv1, pv0), HEAD_DIM, 1)
        if not first:
            alpha = jnp.exp(jnp.where(lo_half, st[:, 2 * g:2 * g + 1] - ms[2 * g],
                                      st[:, 2 * g + 1:2 * g + 2] - ms[2 * g + 1]))
            l_old = jnp.where(lo_half, st[:, H_C + 2 * g:H_C + 2 * g + 1], st[:, H_C + 2 * g + 1:H_C + 2 * g + 2])
            acc_t = acc_t + alpha * acc_in[0, g, rows, :]
            l_t = l_t + alpha * l_old
        if last:
            o_ref[0, :, ts] = acc_t / l_t
        else:
            acc_out[0, g, rows, :] = acc_t
            for e in range(2):
                h = 2 * g + e
                st_new = jnp.where(lane == h, ms[h], st_new)
                st_new = jnp.where(lane == H_C + h, l_t[:, e * HEAD_DIM:e * HEAD_DIM + 1], st_new)
    if not last:
        st_out[0, rows, :] = st_new


def _dilated_branch(qkv_cls, acc, st, *, slopes, first, last):
    B, dil, Lc, W3 = qkv_cls.shape
    W = W3 // 3
    S = dil * Lc
    nj = Lc // TQ
    n_slabs = W // LANES
    assert last == (dil == 1)
    blk = lambda comp, prev: pl.BlockSpec(
        (1, 1, TQ, W), (lambda b, j, r: (b, r, jnp.maximum(j - 1, 0), comp)) if prev
        else (lambda b, j, r: (b, r, j, comp)))
    acc_spec = pl.BlockSpec((1, n_slabs, dil * TQ, LANES), lambda b, j, r: (b, 0, j, 0))
    st_spec = pl.BlockSpec((1, dil * TQ, LANES), lambda b, j, r: (b, j, 0))
    in_specs = [blk(0, False), blk(1, True), blk(1, False), blk(2, True), blk(2, False)]
    args = [qkv_cls] * 5
    if not first:
        in_specs += [acc_spec, st_spec]
        args += [acc, st]
    if last:
        out_shape = jax.ShapeDtypeStruct((B, S, W), f32)
        out_specs = pl.BlockSpec((1, TQ, W), lambda b, j, r: (b, j, 0))
    else:
        out_shape = [jax.ShapeDtypeStruct((B, n_slabs, S, LANES), f32), jax.ShapeDtypeStruct((B, S, LANES), f32)]
        out_specs = [acc_spec, st_spec]
    return pl.pallas_call(
        functools.partial(_dil_kernel, dil=dil, slopes=slopes, first=first, last=last),
        out_shape=out_shape,
        grid=(B, nj, dil),
        in_specs=in_specs,
        out_specs=out_specs,
        scratch_shapes=[pltpu.VMEM((H_C, 2, TQ, TQ), f32),
                        pltpu.VMEM((H_C, 2, TQ, TQ), f32),
                        pltpu.VMEM((H_C, TQ, LANES), f32)],
        compiler_params=_params(("arbitrary", "arbitrary", "arbitrary"), 40),
        name=f"dilated_d{dil}",
    )(*args)


def _dilated(qkv_by_dil, slopes):
    dils = sorted(qkv_by_dil, reverse=True)
    acc = st = None
    for n, dil in enumerate(dils):
        first, last = n == 0, n == len(dils) - 1
        res = _dilated_branch(qkv_by_dil[dil], acc, st, slopes=slopes, first=first, last=last)
        if last:
            return res
        acc, st = res


def _out_kernel(oa_ref, ob_ref, oc_ref, x_ref, gt_ref, go_ref, w_ref, o_ref):
    go = go_ref[...]
    y = None
    o = 0
    for ref in (oa_ref, ob_ref, oc_ref):
        v = ref[0]
        w = v.shape[1]
        ms = jnp.mean(v * v, axis=-1, keepdims=True)
        yn = (v * lax.rsqrt(ms + EPS) * go[:, o:o + w]).astype(bf16)
        part = jnp.dot(yn, w_ref[o:o + w, :], preferred_element_type=f32)
        y = part if y is None else y + part
        o += w
    o_ref[0] = x_ref[0] + gt_ref[0] * y


def _out_proj(oa, ob, oc, x, gt, go, w, *, tm):
    B, S, D = x.shape
    blk = lambda a: pl.BlockSpec((1, tm, a.shape[2]), lambda b, j: (b, j, 0))
    return pl.pallas_call(
        _out_kernel,
        out_shape=jax.ShapeDtypeStruct((B, S, D), f32),
        grid=(B, S // tm),
        in_specs=[blk(oa), blk(ob), blk(oc), blk(x), pl.BlockSpec((1, 1, D), lambda b, j: (b, 0, 0)),
                  _resident(go.shape), _resident(w.shape)],
        out_specs=blk(x),
        compiler_params=_params(("parallel", "parallel"), 40),
        name="mixer_out_proj",
    )(oa, ob, oc, x, gt, go, w)


def _regroup_w_in(w_in):
    WA, WB, WC = H_A * HEAD_DIM, H_B * HEAD_DIM, H_C * HEAD_DIM
    WI = IDX_HEADS * IDX_DIM
    sizes = [WA] * 3 + [WI, IDX_DIM, IDX_HEADS] + [WB] * 3 + [WC] * 3
    offs = np.concatenate([[0], np.cumsum(sizes)])
    seg = [w_in[:, offs[n]:offs[n + 1]] for n in range(len(sizes))]
    qa, ka, va, qi, ki, wi, qb, kb, vb, qc, kc, vc = seg
    pad = jnp.zeros((w_in.shape[0], LANES - IDX_DIM - IDX_HEADS), w_in.dtype)
    return jnp.concatenate([qa, ka, va, qb, kb, vb, qc, kc, vc, qi, ki, wi, pad], axis=1).astype(bf16)


def kernel(x, c, w_ada, b_ada, norm_ffn1, w_gate1, w_up1, w_down1, norm_mix, w_in, qn_a, kn_a, qn_c, kn_c,
           g_out, w_out, norm_ffn2, w_gate2, w_up2, w_down2):
    B, S, D = x.shape
    L = w_ada.shape[0]
    topk = min(TOPK_MAX, S // 4)
    slopes_a, slopes_c = _alibi_slopes()
    tm = min(512, S)
    assert all(window // dil == TQ for window, dil in DILATIONS)
    wide_dils = tuple(sorted(dil for _, dil in DILATIONS if dil > 1))
    mod = _modulation(c, w_ada, b_ada).reshape(L, B, N_MOD, 1, D)
    tile = lambda g, n: jnp.tile(g, n).reshape(1, -1)
    for l in range(L):
        sh1, sc1, g1, sh2, sc2, g2, sh3, sc3, g3 = [mod[l, :, n] for n in range(N_MOD)]
        x = _ffn(x, sh1, sc1, g1, norm_ffn1[l].reshape(1, D), w_gate1[l].astype(bf16), w_up1[l].astype(bf16),
                 w_down1[l].astype(bf16), tm=tm, n_chunks=2)
        qkv_a, qkv_b, qkv_c, qi, kib, kw, *qkv_cls = _proj(
            x, sh2, sc2, norm_mix[l].reshape(1, D), _regroup_w_in(w_in[l]),
            tile(qn_a[l], H_A), tile(kn_a[l], H_A), tile(qn_c[l], H_C), tile(kn_c[l], H_C), tm=tm, dils=wide_dils)
        o_a = _dsa(qkv_a, qi, kib, kw, topk=topk, slopes=slopes_a, ck=min(512, S), tq=min(256, S))
        o_b = _stick_breaking(qkv_b, kb=256, tq=min(256, S))
        o_c = _dilated({1: qkv_c.reshape(B, 1, S, qkv_c.shape[2]), **dict(zip(wide_dils, qkv_cls))}, slopes_c)
        x = _out_proj(o_a, o_b, o_c, x, g2, g_out[l].reshape(1, -1), w_out[l].astype(bf16), tm=tm)
        x = _ffn(x, sh3, sc3, g3, norm_ffn2[l].reshape(1, D), w_gate2[l].astype(bf16), w_up2[l].astype(bf16),
                 w_down2[l].astype(bf16), tm=tm, n_chunks=2)
    return x
```

```python
import functools

import numpy as np
import jax
import jax.numpy as jnp
from jax import lax
from jax.experimental import pallas as pl
from jax.experimental.pallas import tpu as pltpu

HEAD_DIM = 64
H_A = 4
H_B = 4
H_C = 8
IDX_HEADS = 8
IDX_DIM = 64
TOPK_MAX = 256
DILATIONS = ((128, 1), (512, 4), (2048, 16))
N_MOD = 9
EPS = 1e-6

TQ = 128
LANES = 128
ROW_GROUP = 128
NEG = -1e30
INT_MIN = -2 ** 31
NEG_INF_KEY = -2139095041
EXP_DEAD = -104.0
V7X_VMEM_BYTES = 64 * 1024 * 1024

f32 = jnp.float32
bf16 = jnp.bfloat16
i32 = jnp.int32

_NT = (((1,), (1,)), ((), ()))


def _params(sem, vmem_mb):
    return pltpu.CompilerParams(dimension_semantics=sem,
                                vmem_limit_bytes=min(vmem_mb * 1024 * 1024, V7X_VMEM_BYTES - (4 << 20)))


def _alibi_slopes():
    n = H_A + H_C
    s = np.exp2(-8.0 * np.arange(1, n + 1, dtype=np.float32) / np.float32(n)).astype(np.float32)
    idx = np.arange(n)
    return [float(v) for v in s[idx % 3 == 2]], [float(v) for v in s[idx % 3 != 2]]


def _resident(shape):
    nd = len(shape)
    return pl.BlockSpec(shape, lambda *_: (0,) * nd, pipeline_mode=pl.Buffered(1))


def _mod_kernel(c_ref, w_ref, b_ref, o_ref):
    c = c_ref[...]
    cond = c * jax.nn.sigmoid(c)
    o_ref[0] = jnp.dot(cond.astype(bf16), w_ref[0].astype(bf16), preferred_element_type=f32) + b_ref[0]


def _modulation(c, w_ada, b_ada):
    L, D, ND = w_ada.shape
    B = c.shape[0]
    tn = D
    return pl.pallas_call(
        _mod_kernel,
        out_shape=jax.ShapeDtypeStruct((L, B, ND), f32),
        grid=(L, ND // tn),
        in_specs=[pl.BlockSpec((B, D), lambda l, n: (0, 0)),
                  pl.BlockSpec((1, D, tn), lambda l, n: (l, 0, n)),
                  pl.BlockSpec((1, 1, tn), lambda l, n: (l, 0, n))],
        out_specs=pl.BlockSpec((1, B, tn), lambda l, n: (l, 0, n)),
        compiler_params=_params(("arbitrary", "arbitrary"), 32),
        name="adaln_mod",
    )(c, w_ada, b_ada.reshape(L, 1, ND))


def _norm_mod(x, nw, sc, sh):
    ms = jnp.mean(x * x, axis=-1, keepdims=True)
    y = x * lax.rsqrt(ms + EPS) * nw
    return y * (1.0 + sc) + sh


def _ffn_kernel(x_ref, sh_ref, sc_ref, gt_ref, nw_ref, wg_ref, wu_ref, wd_ref, o_ref, *, n_chunks):
    x = x_ref[0]
    h = _norm_mod(x, nw_ref[...], sc_ref[0], sh_ref[0]).astype(bf16)
    F = wg_ref.shape[1]
    fc = F // n_chunks
    y = None
    for c in range(n_chunks):
        g = jnp.dot(h, wg_ref[:, c * fc:(c + 1) * fc], preferred_element_type=f32)
        u = jnp.dot(h, wu_ref[:, c * fc:(c + 1) * fc], preferred_element_type=f32)
        a = (g * jax.nn.sigmoid(g) * u).astype(bf16)
        part = jnp.dot(a, wd_ref[c * fc:(c + 1) * fc, :], preferred_element_type=f32)
        y = part if y is None else y + part
    o_ref[0] = x + (0.5 * gt_ref[0]) * y


def _ffn(x, sh, sc, gt, nw, wg, wu, wd, *, tm, n_chunks):
    B, S, D = x.shape
    F = wg.shape[1]
    vec = pl.BlockSpec((1, 1, D), lambda b, j: (b, 0, 0))
    return pl.pallas_call(
        functools.partial(_ffn_kernel, n_chunks=n_chunks),
        out_shape=jax.ShapeDtypeStruct((B, S, D), f32),
        grid=(B, S // tm),
        in_specs=[pl.BlockSpec((1, tm, D), lambda b, j: (b, j, 0)), vec, vec, vec,
                  _resident((1, D)), _resident((D, F)), _resident((D, F)), _resident((F, D))],
        out_specs=pl.BlockSpec((1, tm, D), lambda b, j: (b, j, 0)),
        compiler_params=_params(("parallel", "parallel"), 56),
        name="ffn_swiglu",
    )(x, sh, sc, gt, nw, wg, wu, wd)


def _head_rms(x, gain_row, scale):
    W = x.shape[1]
    outs = []
    r = lax.broadcasted_iota(i32, (256, 256), 0) >> 6
    c = lax.broadcasted_iota(i32, (256, 256), 1) >> 6
    bd = jnp.where(r == c, 1.0, 0.0).astype(bf16)
    for s in range(W // 256):
        xs = x[:, s * 256:(s + 1) * 256]
        sq = xs * xs
        hi = sq.astype(bf16)
        lo = (sq - hi.astype(f32)).astype(bf16)
        ss = jnp.dot(hi, bd, preferred_element_type=f32) + jnp.dot(lo, bd, preferred_element_type=f32)
        y = xs * lax.rsqrt(ss * (1.0 / HEAD_DIM) + EPS) * gain_row[:, s * 256:(s + 1) * 256]
        outs.append(y * scale if scale != 1.0 else y)
    return outs[0] if len(outs) == 1 else jnp.concatenate(outs, axis=-1)


def _proj_kernel(x_ref, sh_ref, sc_ref, nw_ref, w_ref, qna_ref, kna_ref, qnc_ref, knc_ref,
                 a_ref, b_ref, c_ref, qi_ref, kib_ref, kw_ref, *rest, dils):
    class_refs, slab_ref = rest[:-1], rest[-1]
    tm = x_ref.shape[1]
    x = x_ref[0]
    h = _norm_mod(x, nw_ref[...], sc_ref[0], sh_ref[0]).astype(bf16)
    WA, WB, WC = H_A * HEAD_DIM, H_B * HEAD_DIM, H_C * HEAD_DIM
    qscale = HEAD_DIM ** -0.5
    o = 0

    def mm(width):
        nonlocal o
        r = jnp.dot(h, w_ref[:, o:o + width], preferred_element_type=f32)
        o += width
        return r

    pa = mm(3 * WA)
    a_ref[0] = jnp.concatenate([_head_rms(pa[:, :WA], qna_ref[...], qscale),
                                _head_rms(pa[:, WA:2 * WA], kna_ref[...], 1.0),
                                pa[:, 2 * WA:]], axis=-1).astype(bf16)
    pb = mm(3 * WB)
    b_ref[0] = jnp.concatenate([pb[:, :WB] * qscale, pb[:, WB:]], axis=-1).astype(bf16)
    pc = mm(3 * WC)
    qkv_c = jnp.concatenate([_head_rms(pc[:, :WC], qnc_ref[...], qscale),
                             _head_rms(pc[:, WC:2 * WC], knc_ref[...], 1.0),
                             pc[:, 2 * WC:]], axis=-1)
    c_ref[0] = qkv_c.astype(bf16)
    n_slabs = 3 * WC // LANES
    for g in range(n_slabs):
        slab_ref[g] = qkv_c[:, g * LANES:(g + 1) * LANES]
    for ref, dil in zip(class_refs, dils):
        for r in range(dil):
            for g in range(n_slabs):
                ref[0, r, :, g * LANES:(g + 1) * LANES] = slab_ref[g, pl.ds(r, tm // dil, stride=dil), :].astype(bf16)
    qi_ref[0] = mm(IDX_HEADS * IDX_DIM).astype(bf16)
    kw = mm(LANES)
    kib_ref[0] = kw.astype(bf16)
    kw_ref[0] = kw


def _proj(x, sh, sc, nw, w2, qna, kna, qnc, knc, *, tm, dils):
    B, S, D = x.shape
    WA, WB, WC = H_A * HEAD_DIM, H_B * HEAD_DIM, H_C * HEAD_DIM
    WI = IDX_HEADS * IDX_DIM
    vec = pl.BlockSpec((1, 1, D), lambda b, j: (b, 0, 0))
    widths = (3 * WA, 3 * WB, 3 * WC, WI, LANES, LANES)
    dtypes = (bf16, bf16, bf16, bf16, bf16, f32)
    out_shape = [jax.ShapeDtypeStruct((B, S, w), dt) for w, dt in zip(widths, dtypes)]
    out_specs = [pl.BlockSpec((1, tm, w), lambda b, j: (b, j, 0)) for w in widths]
    for dil in dils:
        out_shape.append(jax.ShapeDtypeStruct((B, dil, S // dil, 3 * WC), bf16))
        out_specs.append(pl.BlockSpec((1, dil, tm // dil, 3 * WC), lambda b, j: (b, 0, j, 0)))
    return pl.pallas_call(
        functools.partial(_proj_kernel, dils=dils),
        out_shape=out_shape,
        grid=(B, S // tm),
        in_specs=[pl.BlockSpec((1, tm, D), lambda b, j: (b, j, 0)), vec, vec,
                  _resident((1, D)), _resident(w2.shape),
                  _resident((1, WA)), _resident((1, WA)), _resident((1, WC)), _resident((1, WC))],
        out_specs=out_specs,
        scratch_shapes=[pltpu.VMEM((3 * WC // LANES, tm, LANES), f32)],
        compiler_params=_params(("parallel", "parallel"), 56),
        name="mixer_in_proj",
    )(x, sh, sc, nw, w2, qna, kna, qnc, knc)


def _dsa_kernel(q_ref, k_ref, v_ref, qi_ref, ki_ref, w_ref, o_ref,
                sc_ref, bias_ref, logit_ref, mx_ref, sum_ref, acc_ref, opnd_ref, *, topk, slopes, ck):
    TQ = q_ref.shape[1]
    S = k_ref.shape[1]
    i = pl.program_id(1)
    t0 = i * TQ
    nck = lax.div(t0 + TQ + ck - 1, ck)
    lt = ck // LANES
    tpos = t0 + lax.broadcasted_iota(i32, (TQ, 1), 0)
    lane_pos = lax.broadcasted_iota(i32, (1, ck), 1)
    idx_scale = (IDX_DIM ** -0.5) * (IDX_HEADS ** -0.5)

    def fold(m):
        r = m[:, 0:LANES]
        for t in range(1, lt):
            r = r + m[:, t * LANES:(t + 1) * LANES]
        return r

    def fold_max(x):
        r = x[:, 0:LANES]
        for t in range(1, lt):
            r = jnp.maximum(r, x[:, t * LANES:(t + 1) * LANES])
        return r

    def from_key(k):
        return lax.bitcast_convert_type(k ^ ((k >> 31) & 0x7FFFFFFF), f32)

    def count(pred_fn, *cols):
        lane_t = lax.broadcasted_iota(i32, (1, LANES), 1)
        groups = [slice(r0, r0 + ROW_GROUP) for r0 in range(0, TQ, ROW_GROUP)]
        for n, col in enumerate(cols):
            opnd_ref[n] = lax.bitcast_convert_type(jnp.broadcast_to(col, (TQ, LANES)), i32)

        def body(c, accs):
            new = []
            for rs, acc in zip(groups, accs):
                tiles = [lax.bitcast_convert_type(opnd_ref[n, rs, :], col.dtype) for n, col in enumerate(cols)]
                sc = sc_ref[c, rs, :]
                for t in range(lt):
                    m = pred_fn(sc[:, t * LANES:(t + 1) * LANES], c * ck + t * LANES + lane_t, *tiles)
                    acc = acc + jnp.where(m, 1, 0).astype(i32)
                new.append(acc)
            return tuple(new)
        accs = lax.fori_loop(0, nck, body, tuple(jnp.zeros((ROW_GROUP, LANES), i32) for _ in groups))
        acc = accs[0] if len(accs) == 1 else jnp.concatenate(accs, axis=0)
        return jnp.sum(acc.astype(f32), axis=-1, keepdims=True)

    qi = qi_ref[0]
    qstack = jnp.concatenate([qi[:, h * IDX_DIM:(h + 1) * IDX_DIM] for h in range(IDX_HEADS)], axis=0)
    wts = w_ref[0]
    wcols = [wts[:, IDX_DIM + h:IDX_DIM + h + 1] for h in range(IDX_HEADS)]

    def score_body(c, carry):
        c0 = pl.multiple_of(c * ck, ck)
        kic = ki_ref[0, pl.ds(c0, ck), :][:, :IDX_DIM]
        rel = lax.dot_general(qstack, kic, _NT, preferred_element_type=f32)
        score = wcols[0] * jnp.maximum(rel[0:TQ], 0.0)
        for h in range(1, IDX_HEADS):
            score = score + wcols[h] * jnp.maximum(rel[h * TQ:(h + 1) * TQ], 0.0)
        score = score * idx_scale
        score = jnp.where(score == 0.0, 0.0, score)
        sc_ref[c] = jnp.where(c0 + lane_pos <= tpos, score, -jnp.inf)
        return carry

    lax.fori_loop(0, nck, score_body, 0)

    kf = float(topk)

    def bit_body(b, p):
        cand_u = p | lax.shift_left(jnp.int32(1), 31 - b)
        cnt = count(lambda sc, kp, cand: sc >= cand, from_key(cand_u ^ INT_MIN))
        return jnp.where(cnt >= kf, cand_u, p)

    tau_key = lax.fori_loop(0, 32, bit_body, jnp.zeros((TQ, 1), i32)) ^ INT_MIN
    tau = from_key(tau_key)
    tau_eff = from_key(jnp.maximum(tau_key, NEG_INF_KEY + 1))

    cnt_gt = count(lambda sc, kp, t: sc > t, tau)
    cnt_ge = count(lambda sc, kp, t: sc >= t, tau)
    need = (cnt_ge > kf) & (tau_key > NEG_INF_KEY)
    any_need = jnp.max(jnp.where(need, 1.0, 0.0)) > 0.0

    @pl.when(jnp.logical_not(any_need))
    def _():
        def bias_body(c, carry):
            bias_ref[c] = jnp.where(sc_ref[c] >= tau_eff, 0.0, NEG)
            return carry

        lax.fori_loop(0, nck, bias_body, 0)

    @pl.when(any_need)
    def _():
        quota = kf - cnt_gt
        nbits = max(1, (S - 1).bit_length())

        def jbit(b, lo):
            cand = lo | lax.shift_left(jnp.int32(1), nbits - 1 - b)
            c = count(lambda sc, kp, t, cd: (sc == t) & (kp < cd), tau, cand)
            return jnp.where(c < quota, cand, lo)

        jstar = lax.fori_loop(0, nbits, jbit, jnp.zeros((TQ, 1), i32))

        def bias_body(c, carry):
            sc = sc_ref[c]
            kp = c * ck + lane_pos
            tied = (sc > tau) | ((sc == tau) & (kp <= jstar))
            plain = jnp.logical_not(need) & (sc >= tau_eff)
            sel = ((need & tied) | plain) & (kp <= tpos)
            bias_ref[c] = jnp.where(sel, 0.0, NEG)
            return carry

        lax.fori_loop(0, nck, bias_body, 0)

    lane = lax.broadcasted_iota(i32, (1, LANES), 1)
    own = [(lane < HEAD_DIM) if h % 2 == 0 else (lane >= HEAD_DIM) for h in range(H_A)]
    qm = []
    for h in range(H_A):
        qt = q_ref[0, :, (h // 2) * LANES:(h // 2 + 1) * LANES]
        qm.append(jnp.where(own[h], qt, jnp.zeros_like(qt)))

    mx_ref[...] = jnp.full(mx_ref.shape, NEG, f32)

    def logit_body(c, carry):
        c0 = pl.multiple_of(c * ck, ck)
        bias = bias_ref[c]
        pos = (c0 + lane_pos).astype(f32)
        for h in range(H_A):
            kt = k_ref[0, pl.ds(c0, ck), (h // 2) * LANES:(h // 2 + 1) * LANES]
            lg = lax.dot_general(qm[h], kt, _NT, preferred_element_type=f32) + (slopes[h] * pos + bias)
            logit_ref[h, c] = lg
            mx_ref[h] = jnp.maximum(mx_ref[h], fold_max(lg))
        return carry

    lax.fori_loop(0, nck, logit_body, 0)
    row_max = [jnp.max(mx_ref[h], axis=-1, keepdims=True) for h in range(H_A)]
    sum_ref[...] = jnp.zeros(sum_ref.shape, f32)
    acc_ref[...] = jnp.zeros(acc_ref.shape, f32)

    def pv_body(c, carry):
        c0 = pl.multiple_of(c * ck, ck)
        for h in range(H_A):
            g = h // 2
            vt = v_ref[0, pl.ds(c0, ck), g * LANES:(g + 1) * LANES]
            p = jnp.exp(logit_ref[h, c] - row_max[h])
            sum_ref[h] = sum_ref[h] + fold(p)
            acc_ref[g] = acc_ref[g] + jnp.dot(p.astype(bf16), jnp.where(own[h], vt, jnp.zeros_like(vt)),
                                              preferred_element_type=f32)
        return carry

    lax.fori_loop(0, nck, pv_body, 0)
    outs = []
    for g in range(H_A // 2):
        l_e = jnp.sum(sum_ref[2 * g], axis=-1, keepdims=True)
        l_o = jnp.sum(sum_ref[2 * g + 1], axis=-1, keepdims=True)
        outs.append(acc_ref[g] / jnp.where(own[2 * g], l_e, l_o))
    o_ref[0] = jnp.concatenate(outs, axis=-1)


def _dsa(qkv, qi, kib, kw, *, topk, slopes, ck, tq):
    B, S, _ = qkv.shape
    W = H_A * HEAD_DIM
    WI = IDX_HEADS * IDX_DIM
    return pl.pallas_call(
        functools.partial(_dsa_kernel, topk=topk, slopes=slopes, ck=ck),
        out_shape=jax.ShapeDtypeStruct((B, S, W), f32),
        grid=(B, S // tq),
        in_specs=[pl.BlockSpec((1, tq, W), lambda b, i: (b, i, 0)),
                  pl.BlockSpec((1, S, W), lambda b, i: (b, 0, 1)),
                  pl.BlockSpec((1, S, W), lambda b, i: (b, 0, 2)),
                  pl.BlockSpec((1, tq, WI), lambda b, i: (b, i, 0)),
                  pl.BlockSpec((1, S, LANES), lambda b, i: (b, 0, 0)),
                  pl.BlockSpec((1, tq, LANES), lambda b, i: (b, i, 0))],
        out_specs=pl.BlockSpec((1, tq, W), lambda b, i: (b, i, 0)),
        scratch_shapes=[pltpu.VMEM((S // ck, tq, ck), f32),
                        pltpu.VMEM((S // ck, tq, ck), f32),
                        pltpu.VMEM((H_A, S // ck, tq, ck), f32),
                        pltpu.VMEM((H_A, tq, LANES), f32),
                        pltpu.VMEM((H_A, tq, LANES), f32),
                        pltpu.VMEM((H_A // 2, tq, LANES), f32),
                        pltpu.VMEM((2, tq, LANES), i32)],
        compiler_params=_params(("parallel", "arbitrary"), 56),
        name="dsa_attention",
    )(qkv, qkv, qkv, qi, kib, kw)


def _sb_kernel(q_ref, k_ref, v_ref, o_ref, *, kb):
    TQ = q_ref.shape[1]
    i = pl.program_id(1)
    t0 = i * TQ
    nkb = lax.div(t0 + TQ + kb - 1, kb)
    tpos = t0 + lax.broadcasted_iota(i32, (TQ, 1), 0)
    lane_pos = lax.broadcasted_iota(i32, (1, kb), 1)
    tri = jnp.where(lax.broadcasted_iota(i32, (kb, kb), 0) > lax.broadcasted_iota(i32, (kb, kb), 1),
                    1.0, 0.0).astype(bf16)
    q = q_ref[0]
    qs = [q[:, h * HEAD_DIM:(h + 1) * HEAD_DIM] for h in range(H_B)]

    def cond(carry):
        jj, alive = carry[0], carry[1]
        return (jj < nkb) & (alive > EXP_DEAD)

    def body(carry):
        jj, _, runs, accs = carry
        s0 = pl.multiple_of((nkb - 1 - jj) * kb, kb)
        kt = k_ref[0, pl.ds(s0, kb), :]
        vt = v_ref[0, pl.ds(s0, kb), :]
        strict = (s0 + lane_pos) < tpos
        zs, sps, lms, parts = [], [], [], ([], [], [])
        for h in range(H_B):
            hs = slice(h * HEAD_DIM, (h + 1) * HEAD_DIM)
            z = lax.dot_general(qs[h], kt[:, hs], _NT, preferred_element_type=f32)
            sp = jnp.maximum(z, 0.0) + jnp.log1p(jnp.exp(-jnp.abs(z)))
            lm = jnp.where(strict, -sp, 0.0)
            l1 = lm.astype(bf16)
            r1 = lm - l1.astype(f32)
            l2 = r1.astype(bf16)
            l3 = (r1 - l2.astype(f32)).astype(bf16)
            zs.append(z); sps.append(sp); lms.append(lm)
            parts[0].append(l1); parts[1].append(l2); parts[2].append(l3)
        later = sum(jnp.dot(jnp.concatenate(p, axis=0), tri, preferred_element_type=f32) for p in parts)
        new_runs, new_accs = [], []
        alive = None
        for h in range(H_B):
            hs = slice(h * HEAD_DIM, (h + 1) * HEAD_DIM)
            a = jnp.where(strict, jnp.exp((zs[h] - sps[h]) + (runs[h] + later[h * TQ:(h + 1) * TQ])), 0.0)
            new_accs.append(accs[h] + jnp.dot(a.astype(bf16), vt[:, hs], preferred_element_type=f32))
            run = runs[h] + jnp.sum(lms[h], axis=-1, keepdims=True)
            new_runs.append(run)
            alive = run if alive is None else jnp.maximum(alive, run)
        return jj + 1, jnp.max(alive), tuple(new_runs), tuple(new_accs)

    init = (jnp.int32(0), jnp.float32(0.0),
            tuple(jnp.zeros((TQ, 1), f32) for _ in range(H_B)),
            tuple(jnp.zeros((TQ, HEAD_DIM), f32) for _ in range(H_B)))
    _, _, _, accs = lax.while_loop(cond, body, init)
    o_ref[0] = jnp.concatenate(accs, axis=-1)


def _stick_breaking(qkv, *, kb, tq):
    B, S, _ = qkv.shape
    W = H_B * HEAD_DIM
    return pl.pallas_call(
        functools.partial(_sb_kernel, kb=kb),
        out_shape=jax.ShapeDtypeStruct((B, S, W), f32),
        grid=(B, S // tq),
        in_specs=[pl.BlockSpec((1, tq, W), lambda b, i: (b, i, 0)),
                  pl.BlockSpec((1, S, W), lambda b, i: (b, 0, 1)),
                  pl.BlockSpec((1, S, W), lambda b, i: (b, 0, 2))],
        out_specs=pl.BlockSpec((1, tq, W), lambda b, i: (b, i, 0)),
        compiler_params=_params(("parallel", "arbitrary"), 32),
        name="stick_breaking",
    )(qkv, qkv, qkv)


def _dil_kernel(*refs, dil, slopes, first, last):
    bias_ref, lg_ref, pv_ref = refs[-3:]
    if first:
        q_ref, kp_ref, kc_ref, vp_ref, vc_ref, acc_out, st_out = refs[:-3]
    elif last:
        q_ref, kp_ref, kc_ref, vp_ref, vc_ref, acc_in, st_in, o_ref = refs[:-3]
    else:
        q_ref, kp_ref, kc_ref, vp_ref, vc_ref, acc_in, st_in, acc_out, st_out = refs[:-3]
    j = pl.program_id(1)
    r = pl.program_id(2)
    rows = pl.ds(r, TQ, stride=dil) if dil > 1 else slice(None)

    @pl.when((pl.program_id(0) == 0) & (j == 0) & (r == 0))
    def _():
        u = lax.broadcasted_iota(i32, (TQ, 1), 0)
        c = lax.broadcasted_iota(i32, (1, TQ), 1)
        d_cur = u - c
        d_prev = d_cur + TQ
        for h in range(H_C):
            bias_ref[h, 0] = jnp.where(d_prev <= TQ, d_prev.astype(f32) * (-slopes[h] * dil), NEG)
            bias_ref[h, 1] = jnp.where(d_cur >= 0, d_cur.astype(f32) * (-slopes[h] * dil), NEG)

    pen = jnp.where(j > 0, 0.0, NEG)
    lane = lax.broadcasted_iota(i32, (TQ, LANES), 1)
    lo_half = lane < HEAD_DIM
    owns = (lo_half, jnp.logical_not(lo_half))

    if not first:
        st = st_in[0, rows, :]

    ms = []
    for h in range(H_C):
        ts = slice((h // 2) * LANES, (h // 2 + 1) * LANES)
        qt = q_ref[0, 0, :, ts]
        qm = jnp.where(owns[h % 2], qt, jnp.zeros_like(qt))
        lg_p = lax.dot_general(qm, kp_ref[0, 0, :, ts], _NT, preferred_element_type=f32) + (bias_ref[h, 0] + pen)
        lg_c = lax.dot_general(qm, kc_ref[0, 0, :, ts], _NT, preferred_element_type=f32) + bias_ref[h, 1]
        lg_ref[h, 0] = lg_p
        lg_ref[h, 1] = lg_c
        m = jnp.max(jnp.maximum(lg_p, lg_c), axis=-1, keepdims=True)
        if not first:
            m = jnp.maximum(m, st[:, h:h + 1])
        ms.append(m)

    for h in range(H_C):
        ts = slice((h // 2) * LANES, (h // 2 + 1) * LANES)
        vpt, vct = vp_ref[0, 0, :, ts], vc_ref[0, 0, :, ts]
        one = jnp.ones_like(vpt)
        p_p = jnp.exp(lg_ref[h, 0] - ms[h]).astype(bf16)
        p_c = jnp.exp(lg_ref[h, 1] - ms[h]).astype(bf16)
        pv_ref[h] = (jnp.dot(p_p, jnp.where(owns[h % 2], vpt, one), preferred_element_type=f32)
                     + jnp.dot(p_c, jnp.where(owns[h % 2], vct, one), preferred_element_type=f32))

    st_new = jnp.zeros((TQ, LANES), f32)
    for g in range(H_C // 2):
        ts = slice(g * LANES, (g + 1) * LANES)
        pv0, pv1 = pv_ref[2 * g], pv_ref[2 * g + 1]
        acc_t = jnp.where(lo_half, pv0, pv1)
        l_t = pltpu.roll(jnp.where(lo_half, pv1, pv0), HEAD_DIM, 1)
        if not first:
            alpha = jnp.exp(jnp.where(lo_half, st[:, 2 * g:2 * g + 1] - ms[2 * g],
                                      st[:, 2 * g + 1:2 * g + 2] - ms[2 * g + 1]))
            l_old = jnp.where(lo_half, st[:, H_C + 2 * g:H_C + 2 * g + 1], st[:, H_C + 2 * g + 1:H_C + 2 * g + 2])
            acc_t = acc_t + alpha * acc_in[0, g, rows, :]
            l_t = l_t + alpha * l_old
        if last:
            o_ref[0, :, ts] = acc_t / l_t
        else:
            acc_out[0, g, rows, :] = acc_t
            for e in range(2):
                h = 2 * g + e
                st_new = jnp.where(lane == h, ms[h], st_new)
                st_new = jnp.where(lane == H_C + h, l_t[:, e * HEAD_DIM:e * HEAD_DIM + 1], st_new)
    if not last:
        st_out[0, rows, :] = st_new


def _dilated_branch(qkv_cls, acc, st, *, slopes, first, last):
    B, dil, Lc, W3 = qkv_cls.shape
    W = W3 // 3
    S = dil * Lc
    nj = Lc // TQ
    n_slabs = W // LANES
    assert last == (dil == 1)
    blk = lambda comp, prev: pl.BlockSpec(
        (1, 1, TQ, W), (lambda b, j, r: (b, r, jnp.maximum(j - 1, 0), comp)) if prev
        else (lambda b, j, r: (b, r, j, comp)))
    acc_spec = pl.BlockSpec((1, n_slabs, dil * TQ, LANES), lambda b, j, r: (b, 0, j, 0))
    st_spec = pl.BlockSpec((1, dil * TQ, LANES), lambda b, j, r: (b, j, 0))
    in_specs = [blk(0, False), blk(1, True), blk(1, False), blk(2, True), blk(2, False)]
    args = [qkv_cls] * 5
    if not first:
        in_specs += [acc_spec, st_spec]
        args += [acc, st]
    if last:
        out_shape = jax.ShapeDtypeStruct((B, S, W), f32)
        out_specs = pl.BlockSpec((1, TQ, W), lambda b, j, r: (b, j, 0))
    else:
        out_shape = [jax.ShapeDtypeStruct((B, n_slabs, S, LANES), f32), jax.ShapeDtypeStruct((B, S, LANES), f32)]
        out_specs = [acc_spec, st_spec]
    return pl.pallas_call(
        functools.partial(_dil_kernel, dil=dil, slopes=slopes, first=first, last=last),
        out_shape=out_shape,
        grid=(B, nj, dil),
        in_specs=in_specs,
        out_specs=out_specs,
        scratch_shapes=[pltpu.VMEM((H_C, 2, TQ, TQ), f32),
                        pltpu.VMEM((H_C, 2, TQ, TQ), f32),
                        pltpu.VMEM((H_C, TQ, LANES), f32)],
        compiler_params=_params(("arbitrary", "arbitrary", "arbitrary"), 40),
        name=f"dilated_d{dil}",
    )(*args)


def _dilated(qkv_by_dil, slopes):
    dils = sorted(qkv_by_dil, reverse=True)
    acc = st = None
    for n, dil in enumerate(dils):
        first, last = n == 0, n == len(dils) - 1
        res = _dilated_branch(qkv_by_dil[dil], acc, st, slopes=slopes, first=first, last=last)
        if last:
            return res
        acc, st = res


def _out_kernel(oa_ref, ob_ref, oc_ref, x_ref, gt_ref, go_ref, w_ref, o_ref):
    go = go_ref[...]
    y = None
    o = 0
    for ref in (oa_ref, ob_ref, oc_ref):
        v = ref[0]
        w = v.shape[1]
        ms = jnp.mean(v * v, axis=-1, keepdims=True)
        yn = (v * lax.rsqrt(ms + EPS) * go[:, o:o + w]).astype(bf16)
        part = jnp.dot(yn, w_ref[o:o + w, :], preferred_element_type=f32)
        y = part if y is None else y + part
        o += w
    o_ref[0] = x_ref[0] + gt_ref[0] * y


def _out_proj(oa, ob, oc, x, gt, go, w, *, tm):
    B, S, D = x.shape
    blk = lambda a: pl.BlockSpec((1, tm, a.shape[2]), lambda b, j: (b, j, 0))
    return pl.pallas_call(
        _out_kernel,
        out_shape=jax.ShapeDtypeStruct((B, S, D), f32),
        grid=(B, S // tm),
        in_specs=[blk(oa), blk(ob), blk(oc), blk(x), pl.BlockSpec((1, 1, D), lambda b, j: (b, 0, 0)),
                  _resident(go.shape), _resident(w.shape)],
        out_specs=blk(x),
        compiler_params=_params(("parallel", "parallel"), 40),
        name="mixer_out_proj",
    )(oa, ob, oc, x, gt, go, w)


def _regroup_w_in(w_in):
    WA, WB, WC = H_A * HEAD_DIM, H_B * HEAD_DIM, H_C * HEAD_DIM
    WI = IDX_HEADS * IDX_DIM
    sizes = [WA] * 3 + [WI, IDX_DIM, IDX_HEADS] + [WB] * 3 + [WC] * 3
    offs = np.concatenate([[0], np.cumsum(sizes)])
    seg = [w_in[:, offs[n]:offs[n + 1]] for n in range(len(sizes))]
    qa, ka, va, qi, ki, wi, qb, kb, vb, qc, kc, vc = seg
    pad = jnp.zeros((w_in.shape[0], LANES - IDX_DIM - IDX_HEADS), w_in.dtype)
    return jnp.concatenate([qa, ka, va, qb, kb, vb, qc, kc, vc, qi, ki, wi, pad], axis=1).astype(bf16)


def kernel(x, c, w_ada, b_ada, norm_ffn1, w_gate1, w_up1, w_down1, norm_mix, w_in, qn_a, kn_a, qn_c, kn_c,
           g_out, w_out, norm_ffn2, w_gate2, w_up2, w_down2):
    B, S, D = x.shape
    L = w_ada.shape[0]
    topk = min(TOPK_MAX, S // 4)
    slopes_a, slopes_c = _alibi_slopes()
    tm = min(512, S)
    assert all(window // dil == TQ for window, dil in DILATIONS)
    wide_dils = tuple(sorted(dil for _, dil in DILATIONS if dil > 1))
    mod = _modulation(c, w_ada, b_ada).reshape(L, B, N_MOD, 1, D)
    tile = lambda g, n: jnp.tile(g, n).reshape(1, -1)
    for l in range(L):
        sh1, sc1, g1, sh2, sc2, g2, sh3, sc3, g3 = [mod[l, :, n] for n in range(N_MOD)]
        x = _ffn(x, sh1, sc1, g1, norm_ffn1[l].reshape(1, D), w_gate1[l].astype(bf16), w_up1[l].astype(bf16),
                 w_down1[l].astype(bf16), tm=tm, n_chunks=2)
        qkv_a, qkv_b, qkv_c, qi, kib, kw, *qkv_cls = _proj(
            x, sh2, sc2, norm_mix[l].reshape(1, D), _regroup_w_in(w_in[l]),
            tile(qn_a[l], H_A), tile(kn_a[l], H_A), tile(qn_c[l], H_C), tile(kn_c[l], H_C), tm=tm, dils=wide_dils)
        o_a = _dsa(qkv_a, qi, kib, kw, topk=topk, slopes=slopes_a, ck=min(512, S), tq=min(256, S))
        o_b = _stick_breaking(qkv_b, kb=256, tq=min(256, S))
        o_c = _dilated({1: qkv_c.reshape(B, 1, S, qkv_c.shape[2]), **dict(zip(wide_dils, qkv_cls))}, slopes_c)
        x = _out_proj(o_a, o_b, o_c, x, g2, g_out[l].reshape(1, -1), w_out[l].astype(bf16), tm=tm)
        x = _ffn(x, sh3, sc3, g3, norm_ffn2[l].reshape(1, D), w_gate2[l].astype(bf16), w_up2[l].astype(bf16),
                 w_down2[l].astype(bf16), tm=tm, n_chunks=2)
    return x
```

```python
import functools

import numpy as np
import jax
import jax.numpy as jnp
from jax import lax
from jax.experimental import pallas as pl
from jax.experimental.pallas import tpu as pltpu

HEAD_DIM = 64
H_A = 4
H_B = 4
H_C = 8
IDX_HEADS = 8
IDX_DIM = 64
TOPK_MAX = 256
DILATIONS = ((128, 1), (512, 4), (2048, 16))
N_MOD = 9
EPS = 1e-6

TQ = 128
LANES = 128
ROW_GROUP = 128
NEG = -1e30
INT_MIN = -2 ** 31
NEG_INF_KEY = -2139095041
EXP_DEAD = -104.0
V7X_VMEM_BYTES = 64 * 1024 * 1024

f32 = jnp.float32
bf16 = jnp.bfloat16
i32 = jnp.int32

_NT = (((1,), (1,)), ((), ()))


def _params(sem, vmem_mb):
    return pltpu.CompilerParams(dimension_semantics=sem,
                                vmem_limit_bytes=min(vmem_mb * 1024 * 1024, V7X_VMEM_BYTES - (4 << 20)))


def _alibi_slopes():
    n = H_A + H_C
    s = np.exp2(-8.0 * np.arange(1, n + 1, dtype=np.float32) / np.float32(n)).astype(np.float32)
    idx = np.arange(n)
    return [float(v) for v in s[idx % 3 == 2]], [float(v) for v in s[idx % 3 != 2]]


def _resident(shape):
    nd = len(shape)
    return pl.BlockSpec(shape, lambda *_: (0,) * nd, pipeline_mode=pl.Buffered(1))


def _mod_kernel(c_ref, w_ref, b_ref, o_ref):
    c = c_ref[...]
    cond = c * jax.nn.sigmoid(c)
    o_ref[0] = jnp.dot(cond.astype(bf16), w_ref[0].astype(bf16), preferred_element_type=f32) + b_ref[0]


def _modulation(c, w_ada, b_ada):
    L, D, ND = w_ada.shape
    B = c.shape[0]
    tn = D
    return pl.pallas_call(
        _mod_kernel,
        out_shape=jax.ShapeDtypeStruct((L, B, ND), f32),
        grid=(L, ND // tn),
        in_specs=[pl.BlockSpec((B, D), lambda l, n: (0, 0)),
                  pl.BlockSpec((1, D, tn), lambda l, n: (l, 0, n)),
                  pl.BlockSpec((1, 1, tn), lambda l, n: (l, 0, n))],
        out_specs=pl.BlockSpec((1, B, tn), lambda l, n: (l, 0, n)),
        compiler_params=_params(("arbitrary", "arbitrary"), 32),
        name="adaln_mod",
    )(c, w_ada, b_ada.reshape(L, 1, ND))


def _norm_mod(x, nw, sc, sh):
    ms = jnp.mean(x * x, axis=-1, keepdims=True)
    y = x * lax.rsqrt(ms + EPS) * nw
    return y * (1.0 + sc) + sh


def _mixer_update(oa_ref, ob_ref, oc_ref, go_ref, w_ref):
    go = go_ref[...]
    y = None
    o = 0
    for ref in (oa_ref, ob_ref, oc_ref):
        v = ref[0]
        w = v.shape[1]
        ms = jnp.mean(v * v, axis=-1, keepdims=True)
        yn = (v * lax.rsqrt(ms + EPS) * go[:, o:o + w]).astype(bf16)
        part = jnp.dot(yn, w_ref[o:o + w, :], preferred_element_type=f32)
        y = part if y is None else y + part
        o += w
    return y


def _ffn_kernel(*refs, n_chunks, mix):
    if mix:
        oa_ref, ob_ref, oc_ref, gt2_ref, go_ref, wo_ref = refs[:6]
        refs = refs[6:]
    x_ref, sh_ref, sc_ref, gt_ref, nw_ref, wg_ref, wu_ref, wd_ref, o_ref = refs
    x = x_ref[0]
    if mix:
        x = x + gt2_ref[0] * _mixer_update(oa_ref, ob_ref, oc_ref, go_ref, wo_ref)
    h = _norm_mod(x, nw_ref[...], sc_ref[0], sh_ref[0]).astype(bf16)
    F = wg_ref.shape[1]
    fc = F // n_chunks
    y = None
    for c in range(n_chunks):
        g = jnp.dot(h, wg_ref[:, c * fc:(c + 1) * fc], preferred_element_type=f32)
        u = jnp.dot(h, wu_ref[:, c * fc:(c + 1) * fc], preferred_element_type=f32)
        a = (g * jax.nn.sigmoid(g) * u).astype(bf16)
        part = jnp.dot(a, wd_ref[c * fc:(c + 1) * fc, :], preferred_element_type=f32)
        y = part if y is None else y + part
    o_ref[0] = x + (0.5 * gt_ref[0]) * y


def _ffn(x, sh, sc, gt, nw, wg, wu, wd, *, tm, n_chunks, mix=None):
    B, S, D = x.shape
    F = wg.shape[1]
    vec = pl.BlockSpec((1, 1, D), lambda b, j: (b, 0, 0))
    rows = lambda a: pl.BlockSpec((1, tm, a.shape[2]), lambda b, j: (b, j, 0))
    in_specs, args = [], []
    if mix is not None:
        oa, ob, oc, gt2, go, wo = mix
        in_specs += [rows(oa), rows(ob), rows(oc), vec, _resident(go.shape), _resident(wo.shape)]
        args += [oa, ob, oc, gt2, go, wo]
    in_specs += [rows(x), vec, vec, vec,
                 _resident((1, D)), _resident((D, F)), _resident((D, F)), _resident((F, D))]
    args += [x, sh, sc, gt, nw, wg, wu, wd]
    return pl.pallas_call(
        functools.partial(_ffn_kernel, n_chunks=n_chunks, mix=mix is not None),
        out_shape=jax.ShapeDtypeStruct((B, S, D), f32),
        grid=(B, S // tm),
        in_specs=in_specs,
        out_specs=rows(x),
        compiler_params=_params(("parallel", "parallel"), 56),
        name="ffn_swiglu_mix" if mix is not None else "ffn_swiglu",
    )(*args)


def _head_rms(x, gain_row, scale):
    W = x.shape[1]
    outs = []
    r = lax.broadcasted_iota(i32, (256, 256), 0) >> 6
    c = lax.broadcasted_iota(i32, (256, 256), 1) >> 6
    bd = jnp.where(r == c, 1.0, 0.0).astype(bf16)
    for s in range(W // 256):
        xs = x[:, s * 256:(s + 1) * 256]
        sq = xs * xs
        hi = sq.astype(bf16)
        lo = (sq - hi.astype(f32)).astype(bf16)
        ss = jnp.dot(hi, bd, preferred_element_type=f32) + jnp.dot(lo, bd, preferred_element_type=f32)
        y = xs * lax.rsqrt(ss * (1.0 / HEAD_DIM) + EPS) * gain_row[:, s * 256:(s + 1) * 256]
        outs.append(y * scale if scale != 1.0 else y)
    return outs[0] if len(outs) == 1 else jnp.concatenate(outs, axis=-1)


def _proj_kernel(x_ref, sh_ref, sc_ref, nw_ref, w_ref, qna_ref, kna_ref, qnc_ref, knc_ref,
                 a_ref, b_ref, c_ref, qi_ref, kib_ref, kw_ref, *rest, dils):
    class_refs, slab_ref = rest[:-1], rest[-1]
    tm = x_ref.shape[1]
    x = x_ref[0]
    h = _norm_mod(x, nw_ref[...], sc_ref[0], sh_ref[0]).astype(bf16)
    WA, WB, WC = H_A * HEAD_DIM, H_B * HEAD_DIM, H_C * HEAD_DIM
    qscale = HEAD_DIM ** -0.5
    o = 0

    def mm(width):
        nonlocal o
        r = jnp.dot(h, w_ref[:, o:o + width], preferred_element_type=f32)
        o += width
        return r

    pa = mm(3 * WA)
    a_ref[0] = jnp.concatenate([_head_rms(pa[:, :WA], qna_ref[...], qscale),
                                _head_rms(pa[:, WA:2 * WA], kna_ref[...], 1.0),
                                pa[:, 2 * WA:]], axis=-1).astype(bf16)
    pb = mm(3 * WB)
    b_ref[0] = jnp.concatenate([pb[:, :WB] * qscale, pb[:, WB:]], axis=-1).astype(bf16)
    pc = mm(3 * WC)
    qkv_c = jnp.concatenate([_head_rms(pc[:, :WC], qnc_ref[...], qscale),
                             _head_rms(pc[:, WC:2 * WC], knc_ref[...], 1.0),
                             pc[:, 2 * WC:]], axis=-1)
    c_ref[0] = qkv_c.astype(bf16)
    n_slabs = 3 * WC // LANES
    for g in range(n_slabs):
        slab_ref[g] = qkv_c[:, g * LANES:(g + 1) * LANES]
    for ref, dil in zip(class_refs, dils):
        for r in range(dil):
            for g in range(n_slabs):
                ref[0, r, :, g * LANES:(g + 1) * LANES] = slab_ref[g, pl.ds(r, tm // dil, stride=dil), :].astype(bf16)
    qi_ref[0] = mm(IDX_HEADS * IDX_DIM).astype(bf16)
    kw = mm(LANES)
    kib_ref[0] = kw.astype(bf16)
    kw_ref[0] = kw


def _proj(x, sh, sc, nw, w2, qna, kna, qnc, knc, *, tm, dils):
    B, S, D = x.shape
    WA, WB, WC = H_A * HEAD_DIM, H_B * HEAD_DIM, H_C * HEAD_DIM
    WI = IDX_HEADS * IDX_DIM
    vec = pl.BlockSpec((1, 1, D), lambda b, j: (b, 0, 0))
    widths = (3 * WA, 3 * WB, 3 * WC, WI, LANES, LANES)
    dtypes = (bf16, bf16, bf16, bf16, bf16, f32)
    out_shape = [jax.ShapeDtypeStruct((B, S, w), dt) for w, dt in zip(widths, dtypes)]
    out_specs = [pl.BlockSpec((1, tm, w), lambda b, j: (b, j, 0)) for w in widths]
    for dil in dils:
        out_shape.append(jax.ShapeDtypeStruct((B, dil, S // dil, 3 * WC), bf16))
        out_specs.append(pl.BlockSpec((1, dil, tm // dil, 3 * WC), lambda b, j: (b, 0, j, 0)))
    return pl.pallas_call(
        functools.partial(_proj_kernel, dils=dils),
        out_shape=out_shape,
        grid=(B, S // tm),
        in_specs=[pl.BlockSpec((1, tm, D), lambda b, j: (b, j, 0)), vec, vec,
                  _resident((1, D)), _resident(w2.shape),
                  _resident((1, WA)), _resident((1, WA)), _resident((1, WC)), _resident((1, WC))],
        out_specs=out_specs,
        scratch_shapes=[pltpu.VMEM((3 * WC // LANES, tm, LANES), f32)],
        compiler_params=_params(("parallel", "parallel"), 56),
        name="mixer_in_proj",
    )(x, sh, sc, nw, w2, qna, kna, qnc, knc)


def _dsa_kernel(q_ref, k_ref, v_ref, qi_ref, ki_ref, w_ref, o_ref,
                sc_ref, bias_ref, logit_ref, mx_ref, sum_ref, acc_ref, opnd_ref, *, topk, slopes, ck):
    TQ = q_ref.shape[1]
    S = k_ref.shape[1]
    i = pl.program_id(1)
    t0 = i * TQ
    nck = lax.div(t0 + TQ + ck - 1, ck)
    lt = ck // LANES
    tpos = t0 + lax.broadcasted_iota(i32, (TQ, 1), 0)
    lane_pos = lax.broadcasted_iota(i32, (1, ck), 1)
    idx_scale = (IDX_DIM ** -0.5) * (IDX_HEADS ** -0.5)

    def fold(m):
        r = m[:, 0:LANES]
        for t in range(1, lt):
            r = r + m[:, t * LANES:(t + 1) * LANES]
        return r

    def fold_max(x):
        r = x[:, 0:LANES]
        for t in range(1, lt):
            r = jnp.maximum(r, x[:, t * LANES:(t + 1) * LANES])
        return r

    def from_key(k):
        return lax.bitcast_convert_type(k ^ ((k >> 31) & 0x7FFFFFFF), f32)

    def count(pred_fn, *cols):
        lane_t = lax.broadcasted_iota(i32, (1, LANES), 1)
        groups = [slice(r0, r0 + ROW_GROUP) for r0 in range(0, TQ, ROW_GROUP)]
        for n, col in enumerate(cols):
            opnd_ref[n] = lax.bitcast_convert_type(jnp.broadcast_to(col, (TQ, LANES)), i32)

        def body(c, accs):
            new = []
            for rs, acc in zip(groups, accs):
                tiles = [lax.bitcast_convert_type(opnd_ref[n, rs, :], col.dtype) for n, col in enumerate(cols)]
                sc = sc_ref[c, rs, :]
                for t in range(lt):
                    m = pred_fn(sc[:, t * LANES:(t + 1) * LANES], c * ck + t * LANES + lane_t, *tiles)
                    acc = acc + jnp.where(m, 1, 0).astype(i32)
                new.append(acc)
            return tuple(new)
        accs = lax.fori_loop(0, nck, body, tuple(jnp.zeros((ROW_GROUP, LANES), i32) for _ in groups))
        acc = accs[0] if len(accs) == 1 else jnp.concatenate(accs, axis=0)
        return jnp.sum(acc.astype(f32), axis=-1, keepdims=True)

    qi = qi_ref[0]
    qstack = jnp.concatenate([qi[:, h * IDX_DIM:(h + 1) * IDX_DIM] for h in range(IDX_HEADS)], axis=0)
    wts = w_ref[0]
    wcols = [wts[:, IDX_DIM + h:IDX_DIM + h + 1] for h in range(IDX_HEADS)]

    def score_body(c, carry):
        c0 = pl.multiple_of(c * ck, ck)
        kic = ki_ref[0, pl.ds(c0, ck), :][:, :IDX_DIM]
        rel = lax.dot_general(qstack, kic, _NT, preferred_element_type=f32)
        score = wcols[0] * jnp.maximum(rel[0:TQ], 0.0)
        for h in range(1, IDX_HEADS):
            score = score + wcols[h] * jnp.maximum(rel[h * TQ:(h + 1) * TQ], 0.0)
        score = score * idx_scale
        score = jnp.where(score == 0.0, 0.0, score)
        sc_ref[c] = jnp.where(c0 + lane_pos <= tpos, score, -jnp.inf)
        return carry

    lax.fori_loop(0, nck, score_body, 0)

    kf = float(topk)

    def bit_body(b, p):
        cand_u = p | lax.shift_left(jnp.int32(1), 31 - b)
        cnt = count(lambda sc, kp, cand: sc >= cand, from_key(cand_u ^ INT_MIN))
        return jnp.where(cnt >= kf, cand_u, p)

    tau_key = lax.fori_loop(0, 32, bit_body, jnp.zeros((TQ, 1), i32)) ^ INT_MIN
    tau = from_key(tau_key)
    tau_eff = from_key(jnp.maximum(tau_key, NEG_INF_KEY + 1))

    cnt_gt = count(lambda sc, kp, t: sc > t, tau)
    cnt_ge = count(lambda sc, kp, t: sc >= t, tau)
    need = (cnt_ge > kf) & (tau_key > NEG_INF_KEY)
    any_need = jnp.max(jnp.where(need, 1.0, 0.0)) > 0.0

    @pl.when(jnp.logical_not(any_need))
    def _():
        def bias_body(c, carry):
            bias_ref[c] = jnp.where(sc_ref[c] >= tau_eff, 0.0, NEG)
            return carry

        lax.fori_loop(0, nck, bias_body, 0)

    @pl.when(any_need)
    def _():
        quota = kf - cnt_gt
        nbits = max(1, (S - 1).bit_length())

        def jbit(b, lo):
            cand = lo | lax.shift_left(jnp.int32(1), nbits - 1 - b)
            c = count(lambda sc, kp, t, cd: (sc == t) & (kp < cd), tau, cand)
            return jnp.where(c < quota, cand, lo)

        jstar = lax.fori_loop(0, nbits, jbit, jnp.zeros((TQ, 1), i32))

        def bias_body(c, carry):
            sc = sc_ref[c]
            kp = c * ck + lane_pos
            tied = (sc > tau) | ((sc == tau) & (kp <= jstar))
            plain = jnp.logical_not(need) & (sc >= tau_eff)
            sel = ((need & tied) | plain) & (kp <= tpos)
            bias_ref[c] = jnp.where(sel, 0.0, NEG)
            return carry

        lax.fori_loop(0, nck, bias_body, 0)

    lane = lax.broadcasted_iota(i32, (1, LANES), 1)
    own = [(lane < HEAD_DIM) if h % 2 == 0 else (lane >= HEAD_DIM) for h in range(H_A)]
    qm = []
    for h in range(H_A):
        qt = q_ref[0, :, (h // 2) * LANES:(h // 2 + 1) * LANES]
        qm.append(jnp.where(own[h], qt, jnp.zeros_like(qt)))

    mx_ref[...] = jnp.full(mx_ref.shape, NEG, f32)

    def logit_body(c, carry):
        c0 = pl.multiple_of(c * ck, ck)
        bias = bias_ref[c]
        pos = (c0 + lane_pos).astype(f32)
        for h in range(H_A):
            kt = k_ref[0, pl.ds(c0, ck), (h // 2) * LANES:(h // 2 + 1) * LANES]
            lg = lax.dot_general(qm[h], kt, _NT, preferred_element_type=f32) + (slopes[h] * pos + bias)
            logit_ref[h, c] = lg
            mx_ref[h] = jnp.maximum(mx_ref[h], fold_max(lg))
        return carry

    lax.fori_loop(0, nck, logit_body, 0)
    row_max = [jnp.max(mx_ref[h], axis=-1, keepdims=True) for h in range(H_A)]
    sum_ref[...] = jnp.zeros(sum_ref.shape, f32)
    acc_ref[...] = jnp.zeros(acc_ref.shape, f32)

    def pv_body(c, carry):
        c0 = pl.multiple_of(c * ck, ck)
        for h in range(H_A):
            g = h // 2
            vt = v_ref[0, pl.ds(c0, ck), g * LANES:(g + 1) * LANES]
            p = jnp.exp(logit_ref[h, c] - row_max[h])
            sum_ref[h] = sum_ref[h] + fold(p)
            acc_ref[g] = acc_ref[g] + jnp.dot(p.astype(bf16), jnp.where(own[h], vt, jnp.zeros_like(vt)),
                                              preferred_element_type=f32)
        return carry

    lax.fori_loop(0, nck, pv_body, 0)
    outs = []
    for g in range(H_A // 2):
        l_e = jnp.sum(sum_ref[2 * g], axis=-1, keepdims=True)
        l_o = jnp.sum(sum_ref[2 * g + 1], axis=-1, keepdims=True)
        outs.append(acc_ref[g] / jnp.where(own[2 * g], l_e, l_o))
    o_ref[0] = jnp.concatenate(outs, axis=-1)


def _dsa(qkv, qi, kib, kw, *, topk, slopes, ck, tq):
    B, S, _ = qkv.shape
    W = H_A * HEAD_DIM
    WI = IDX_HEADS * IDX_DIM
    return pl.pallas_call(
        functools.partial(_dsa_kernel, topk=topk, slopes=slopes, ck=ck),
        out_shape=jax.ShapeDtypeStruct((B, S, W), f32),
        grid=(B, S // tq),
        in_specs=[pl.BlockSpec((1, tq, W), lambda b, i: (b, i, 0)),
                  pl.BlockSpec((1, S, W), lambda b, i: (b, 0, 1)),
                  pl.BlockSpec((1, S, W), lambda b, i: (b, 0, 2)),
                  pl.BlockSpec((1, tq, WI), lambda b, i: (b, i, 0)),
                  pl.BlockSpec((1, S, LANES), lambda b, i: (b, 0, 0)),
                  pl.BlockSpec((1, tq, LANES), lambda b, i: (b, i, 0))],
        out_specs=pl.BlockSpec((1, tq, W), lambda b, i: (b, i, 0)),
        scratch_shapes=[pltpu.VMEM((S // ck, tq, ck), f32),
                        pltpu.VMEM((S // ck, tq, ck), f32),
                        pltpu.VMEM((H_A, S // ck, tq, ck), f32),
                        pltpu.VMEM((H_A, tq, LANES), f32),
                        pltpu.VMEM((H_A, tq, LANES), f32),
                        pltpu.VMEM((H_A // 2, tq, LANES), f32),
                        pltpu.VMEM((2, tq, LANES), i32)],
        compiler_params=_params(("parallel", "arbitrary"), 56),
        name="dsa_attention",
    )(qkv, qkv, qkv, qi, kib, kw)


def _sb_kernel(q_ref, k_ref, v_ref, o_ref, *, kb):
    TQ = q_ref.shape[1]
    i = pl.program_id(1)
    t0 = i * TQ
    nkb = lax.div(t0 + TQ + kb - 1, kb)
    tpos = t0 + lax.broadcasted_iota(i32, (TQ, 1), 0)
    lane_pos = lax.broadcasted_iota(i32, (1, kb), 1)
    tri = jnp.where(lax.broadcasted_iota(i32, (kb, kb), 0) > lax.broadcasted_iota(i32, (kb, kb), 1),
                    1.0, 0.0).astype(bf16)
    q = q_ref[0]
    qs = [q[:, h * HEAD_DIM:(h + 1) * HEAD_DIM] for h in range(H_B)]

    def cond(carry):
        jj, alive = carry[0], carry[1]
        return (jj < nkb) & (alive > EXP_DEAD)

    def body(carry):
        jj, _, runs, accs = carry
        s0 = pl.multiple_of((nkb - 1 - jj) * kb, kb)
        kt = k_ref[0, pl.ds(s0, kb), :]
        vt = v_ref[0, pl.ds(s0, kb), :]
        strict = (s0 + lane_pos) < tpos
        zs, sps, lms, parts = [], [], [], ([], [], [])
        for h in range(H_B):
            hs = slice(h * HEAD_DIM, (h + 1) * HEAD_DIM)
            z = lax.dot_general(qs[h], kt[:, hs], _NT, preferred_element_type=f32)
            sp = jnp.maximum(z, 0.0) + jnp.log1p(jnp.exp(-jnp.abs(z)))
            lm = jnp.where(strict, -sp, 0.0)
            l1 = lm.astype(bf16)
            r1 = lm - l1.astype(f32)
            l2 = r1.astype(bf16)
            l3 = (r1 - l2.astype(f32)).astype(bf16)
            zs.append(z); sps.append(sp); lms.append(lm)
            parts[0].append(l1); parts[1].append(l2); parts[2].append(l3)
        later = sum(jnp.dot(jnp.concatenate(p, axis=0), tri, preferred_element_type=f32) for p in parts)
        new_runs, new_accs = [], []
        alive = None
        for h in range(H_B):
            hs = slice(h * HEAD_DIM, (h + 1) * HEAD_DIM)
            a = jnp.where(strict, jnp.exp((zs[h] - sps[h]) + (runs[h] + later[h * TQ:(h + 1) * TQ])), 0.0)
            new_accs.append(accs[h] + jnp.dot(a.astype(bf16), vt[:, hs], preferred_element_type=f32))
            run = runs[h] + jnp.sum(lms[h], axis=-1, keepdims=True)
            new_runs.append(run)
            alive = run if alive is None else jnp.maximum(alive, run)
        return jj + 1, jnp.max(alive), tuple(new_runs), tuple(new_accs)

    init = (jnp.int32(0), jnp.float32(0.0),
            tuple(jnp.zeros((TQ, 1), f32) for _ in range(H_B)),
            tuple(jnp.zeros((TQ, HEAD_DIM), f32) for _ in range(H_B)))
    _, _, _, accs = lax.while_loop(cond, body, init)
    o_ref[0] = jnp.concatenate(accs, axis=-1)


def _stick_breaking(qkv, *, kb, tq):
    B, S, _ = qkv.shape
    W = H_B * HEAD_DIM
    return pl.pallas_call(
        functools.partial(_sb_kernel, kb=kb),
        out_shape=jax.ShapeDtypeStruct((B, S, W), f32),
        grid=(B, S // tq),
        in_specs=[pl.BlockSpec((1, tq, W), lambda b, i: (b, i, 0)),
                  pl.BlockSpec((1, S, W), lambda b, i: (b, 0, 1)),
                  pl.BlockSpec((1, S, W), lambda b, i: (b, 0, 2))],
        out_specs=pl.BlockSpec((1, tq, W), lambda b, i: (b, i, 0)),
        compiler_params=_params(("parallel", "arbitrary"), 32),
        name="stick_breaking",
    )(qkv, qkv, qkv)


def _dil_kernel(*refs, dil, slopes, first, last):
    bias_ref, lg_ref, pv_ref = refs[-3:]
    if first:
        q_ref, kp_ref, kc_ref, vp_ref, vc_ref, acc_out, st_out = refs[:-3]
    elif last:
        q_ref, kp_ref, kc_ref, vp_ref, vc_ref, acc_in, st_in, o_ref = refs[:-3]
    else:
        q_ref, kp_ref, kc_ref, vp_ref, vc_ref, acc_in, st_in, acc_out, st_out = refs[:-3]
    j = pl.program_id(1)
    r = pl.program_id(2)
    rows = pl.ds(r, TQ, stride=dil) if dil > 1 else slice(None)

    @pl.when((pl.program_id(0) == 0) & (j == 0) & (r == 0))
    def _():
        u = lax.broadcasted_iota(i32, (TQ, 1), 0)
        c = lax.broadcasted_iota(i32, (1, TQ), 1)
        d_cur = u - c
        d_prev = d_cur + TQ
        for h in range(H_C):
            bias_ref[h, 0] = jnp.where(d_prev <= TQ, d_prev.astype(f32) * (-slopes[h] * dil), NEG)
            bias_ref[h, 1] = jnp.where(d_cur >= 0, d_cur.astype(f32) * (-slopes[h] * dil), NEG)

    pen = jnp.where(j > 0, 0.0, NEG)
    lane = lax.broadcasted_iota(i32, (TQ, LANES), 1)
    lo_half = lane < HEAD_DIM
    owns = (lo_half, jnp.logical_not(lo_half))

    if not first:
        st = st_in[0, rows, :]

    ms = []
    for h in range(H_C):
        ts = slice((h // 2) * LANES, (h // 2 + 1) * LANES)
        qt = q_ref[0, 0, :, ts]
        qm = jnp.where(owns[h % 2], qt, jnp.zeros_like(qt))
        lg_p = lax.dot_general(qm, kp_ref[0, 0, :, ts], _NT, preferred_element_type=f32) + (bias_ref[h, 0] + pen)
        lg_c = lax.dot_general(qm, kc_ref[0, 0, :, ts], _NT, preferred_element_type=f32) + bias_ref[h, 1]
        lg_ref[h, 0] = lg_p
        lg_ref[h, 1] = lg_c
        m = jnp.max(jnp.maximum(lg_p, lg_c), axis=-1, keepdims=True)
        if not first:
            m = jnp.maximum(m, st[:, h:h + 1])
        ms.append(m)

    for h in range(H_C):
        ts = slice((h // 2) * LANES, (h // 2 + 1) * LANES)
        vpt, vct = vp_ref[0, 0, :, ts], vc_ref[0, 0, :, ts]
        one = jnp.ones_like(vpt)
        p_p = jnp.exp(lg_ref[h, 0] - ms[h]).astype(bf16)
        p_c = jnp.exp(lg_ref[h, 1] - ms[h]).astype(bf16)
        pv_ref[h] = (jnp.dot(p_p, jnp.where(owns[h % 2], vpt, one), preferred_element_type=f32)
                     + jnp.dot(p_c, jnp.where(owns[h % 2], vct, one), preferred_element_type=f32))

    st_new = jnp.zeros((TQ, LANES), f32)
    for g in range(H_C // 2):
        ts = slice(g * LANES, (g + 1) * LANES)
        pv0, pv1 = pv_ref[2 * g], pv_ref[2 * g + 1]
        acc_t = jnp.where(lo_half, pv0, pv1)
        l_t = pltpu.roll(jnp.where(lo_half, pv1, pv0), HEAD_DIM, 1)
        if not first:
            alpha = jnp.exp(jnp.where(lo_half, st[:, 2 * g:2 * g + 1] - ms[2 * g],
                                      st[:, 2 * g + 1:2 * g + 2] - ms[2 * g + 1]))
            l_old = jnp.where(lo_half, st[:, H_C + 2 * g:H_C + 2 * g + 1], st[:, H_C + 2 * g + 1:H_C + 2 * g + 2])
            acc_t = acc_t + alpha * acc_in[0, g, rows, :]
            l_t = l_t + alpha * l_old
        if last:
            o_ref[0, :, ts] = acc_t / l_t
        else:
            acc_out[0, g, rows, :] = acc_t
            for e in range(2):
                h = 2 * g + e
                st_new = jnp.where(lane == h, ms[h], st_new)
                st_new = jnp.where(lane == H_C + h, l_t[:, e * HEAD_DIM:e * HEAD_DIM + 1], st_new)
    if not last:
        st_out[0, rows, :] = st_new


def _dilated_branch(qkv_cls, acc, st, *, slopes, first, last):
    B, dil, Lc, W3 = qkv_cls.shape
    W = W3 // 3
    S = dil * Lc
    nj = Lc // TQ
    n_slabs = W // LANES
    assert last == (dil == 1)
    blk = lambda comp, prev: pl.BlockSpec(
        (1, 1, TQ, W), (lambda b, j, r: (b, r, jnp.maximum(j - 1, 0), comp)) if prev
        else (lambda b, j, r: (b, r, j, comp)))
    acc_spec = pl.BlockSpec((1, n_slabs, dil * TQ, LANES), lambda b, j, r: (b, 0, j, 0))
    st_spec = pl.BlockSpec((1, dil * TQ, LANES), lambda b, j, r: (b, j, 0))
    in_specs = [blk(0, False), blk(1, True), blk(1, False), blk(2, True), blk(2, False)]
    args = [qkv_cls] * 5
    if not first:
        in_specs += [acc_spec, st_spec]
        args += [acc, st]
    if last:
        out_shape = jax.ShapeDtypeStruct((B, S, W), f32)
        out_specs = pl.BlockSpec((1, TQ, W), lambda b, j, r: (b, j, 0))
    else:
        out_shape = [jax.ShapeDtypeStruct((B, n_slabs, S, LANES), f32), jax.ShapeDtypeStruct((B, S, LANES), f32)]
        out_specs = [acc_spec, st_spec]
    return pl.pallas_call(
        functools.partial(_dil_kernel, dil=dil, slopes=slopes, first=first, last=last),
        out_shape=out_shape,
        grid=(B, nj, dil),
        in_specs=in_specs,
        out_specs=out_specs,
        scratch_shapes=[pltpu.VMEM((H_C, 2, TQ, TQ), f32),
                        pltpu.VMEM((H_C, 2, TQ, TQ), f32),
                        pltpu.VMEM((H_C, TQ, LANES), f32)],
        compiler_params=_params(("arbitrary", "arbitrary", "arbitrary"), 40),
        name=f"dilated_d{dil}",
    )(*args)


def _dilated(qkv_by_dil, slopes):
    dils = sorted(qkv_by_dil, reverse=True)
    acc = st = None
    for n, dil in enumerate(dils):
        first, last = n == 0, n == len(dils) - 1
        res = _dilated_branch(qkv_by_dil[dil], acc, st, slopes=slopes, first=first, last=last)
        if last:
            return res
        acc, st = res


def _regroup_w_in(w_in):
    WA, WB, WC = H_A * HEAD_DIM, H_B * HEAD_DIM, H_C * HEAD_DIM
    WI = IDX_HEADS * IDX_DIM
    sizes = [WA] * 3 + [WI, IDX_DIM, IDX_HEADS] + [WB] * 3 + [WC] * 3
    offs = np.concatenate([[0], np.cumsum(sizes)])
    seg = [w_in[:, offs[n]:offs[n + 1]] for n in range(len(sizes))]
    qa, ka, va, qi, ki, wi, qb, kb, vb, qc, kc, vc = seg
    pad = jnp.zeros((w_in.shape[0], LANES - IDX_DIM - IDX_HEADS), w_in.dtype)
    return jnp.concatenate([qa, ka, va, qb, kb, vb, qc, kc, vc, qi, ki, wi, pad], axis=1).astype(bf16)


def kernel(x, c, w_ada, b_ada, norm_ffn1, w_gate1, w_up1, w_down1, norm_mix, w_in, qn_a, kn_a, qn_c, kn_c,
           g_out, w_out, norm_ffn2, w_gate2, w_up2, w_down2):
    B, S, D = x.shape
    L = w_ada.shape[0]
    topk = min(TOPK_MAX, S // 4)
    slopes_a, slopes_c = _alibi_slopes()
    tm = min(512, S)
    assert all(window // dil == TQ for window, dil in DILATIONS)
    wide_dils = tuple(sorted(dil for _, dil in DILATIONS if dil > 1))
    mod = _modulation(c, w_ada, b_ada).reshape(L, B, N_MOD, 1, D)
    tile = lambda g, n: jnp.tile(g, n).reshape(1, -1)
    for l in range(L):
        sh1, sc1, g1, sh2, sc2, g2, sh3, sc3, g3 = [mod[l, :, n] for n in range(N_MOD)]
        x = _ffn(x, sh1, sc1, g1, norm_ffn1[l].reshape(1, D), w_gate1[l].astype(bf16), w_up1[l].astype(bf16),
                 w_down1[l].astype(bf16), tm=tm, n_chunks=2)
        qkv_a, qkv_b, qkv_c, qi, kib, kw, *qkv_cls = _proj(
            x, sh2, sc2, norm_mix[l].reshape(1, D), _regroup_w_in(w_in[l]),
            tile(qn_a[l], H_A), tile(kn_a[l], H_A), tile(qn_c[l], H_C), tile(kn_c[l], H_C), tm=tm, dils=wide_dils)
        o_a = _dsa(qkv_a, qi, kib, kw, topk=topk, slopes=slopes_a, ck=min(512, S), tq=min(256, S))
        o_b = _stick_breaking(qkv_b, kb=256, tq=min(256, S))
        o_c = _dilated({1: qkv_c.reshape(B, 1, S, qkv_c.shape[2]), **dict(zip(wide_dils, qkv_cls))}, slopes_c)
        x = _ffn(x, sh3, sc3, g3, norm_ffn2[l].reshape(1, D), w_gate2[l].astype(bf16), w_up2[l].astype(bf16),
                 w_down2[l].astype(bf16), tm=tm, n_chunks=2,
                 mix=(o_a, o_b, o_c, g2, g_out[l].reshape(1, -1), w_out[l].astype(bf16)))
    return x
```
